```python
import functools
import jax, jax.numpy as jnp
from jax import lax
import numpy as np

D_MODEL = 1024
BATCH = 2
SEQ = 8192
DEPTH = 4
DEC_BATCH = 128
DEC_SEQ = 8
PAST_LEN = 2048
PAGE_SIZE = 128

N_HEADS = 8
HEAD_DIM = 64
N_KV_HEADS = 2
KV_GROUP = N_HEADS // N_KV_HEADS
N_IDX_HEADS = 4
IDX_DIM = 64
TOPK_MAX = 256
ROPE_THETA = 10000.0
Q_BLOCK = 128
CONV_DIM = 512
CONV_W = 3
BRANCH_DIM = N_HEADS * HEAD_DIM
N_BRANCHES = 2
N_EXPERTS = 16
N_GROUPS = 4
EXPERTS_PER_GROUP = N_EXPERTS // N_GROUPS
TOP_K_EXPERTS = 2
D_EXPERT = 256
LN_EPS = 1e-5
DEEPNORM_ALPHA = (2 * DEPTH) ** 0.25
DEEPNORM_BETA = (8 * DEPTH) ** -0.25
NEG_INF = -1e30
SPLIT_SIZES = (N_HEADS * HEAD_DIM, N_KV_HEADS * HEAD_DIM, N_KV_HEADS * HEAD_DIM, N_IDX_HEADS * IDX_DIM, IDX_DIM, N_IDX_HEADS, CONV_DIM, CONV_DIM, CONV_DIM, D_MODEL, D_MODEL)
IN_WIDTH = sum(SPLIT_SIZES)

kernel_name = 'hybrid_dsa_shortconv_groupmoe_step'


def split_proj(z):
    offsets = np.cumsum(np.array(SPLIT_SIZES))[:-1].tolist()
    return jnp.split(z, offsets, axis=-1)


def rope(x, pos):
    d = x.shape[-1]
    half = d // 2
    inv = 1.0 / (ROPE_THETA ** (jnp.arange(half, dtype=jnp.float32) / half))
    ang = pos.astype(jnp.float32)[:, None] * inv[None, :]
    cos = jnp.cos(ang)[:, None, :]
    sin = jnp.sin(ang)[:, None, :]
    xf = x.astype(jnp.float32)
    x1, x2 = xf[..., :half], xf[..., half:]
    return jnp.concatenate([x1 * cos - x2 * sin, x2 * cos + x1 * sin], axis=-1).astype(x.dtype)


def layer_norm(x, g, b):
    xf = x.astype(jnp.float32)
    mu = xf.mean(-1, keepdims=True)
    var = jnp.square(xf - mu).mean(-1, keepdims=True)
    return ((xf - mu) * lax.rsqrt(var + LN_EPS) * g.astype(jnp.float32) + b.astype(jnp.float32)).astype(x.dtype)


def indexer_scores(iq, iw, ik):
    dots = jnp.einsum('bqhd,bsd->bqhs', iq.astype(jnp.float32), ik.astype(jnp.float32)) * IDX_DIM ** -0.5
    return jnp.einsum('bqhs,bqh->bqs', jax.nn.relu(dots), iw.astype(jnp.float32))


def select_keys(scores, q_pos, n_sel):
    key_pos = jnp.arange(scores.shape[-1], dtype=jnp.int32)
    allowed = key_pos[None, :] <= q_pos[:, None]
    scores = jnp.where(allowed[None], scores, NEG_INF)
    _, idx = lax.top_k(scores, n_sel)
    valid = idx <= q_pos[None, :, None]
    return idx, valid


def attend_selected(q, k_sel, v_sel, valid):
    b, nq = q.shape[:2]
    qg = q.reshape(b, nq, N_KV_HEADS, KV_GROUP, HEAD_DIM).astype(jnp.float32)
    logits = jnp.einsum('bqkgd,bqskd->bqkgs', qg, k_sel.astype(jnp.float32)) * HEAD_DIM ** -0.5
    logits = jnp.where(valid[:, :, None, None, :], logits, NEG_INF)
    p = jax.nn.softmax(logits, axis=-1)
    o = jnp.einsum('bqkgs,bqskd->bqkgd', p, v_sel.astype(jnp.float32))
    return o.reshape(b, nq, N_HEADS * HEAD_DIM).astype(q.dtype)


def gather_rows(a, idx):
    return jax.vmap(lambda ab, ib: ab[ib])(a, idx)


def prompt_sparse_attention(q, k, v, iq, iw, ik):
    b, s = q.shape[:2]
    n_sel = min(TOPK_MAX, s // 4)
    nb = s // Q_BLOCK

    def blocks(a):
        return jnp.moveaxis(a.reshape((b, nb, Q_BLOCK) + a.shape[2:]), 1, 0)

    pos_blocks = jnp.arange(s, dtype=jnp.int32).reshape(nb, Q_BLOCK)

    def one_block(args):
        qb, iqb, iwb, posb = args
        sc = indexer_scores(iqb, iwb, ik)
        idx, valid = select_keys(sc, posb, n_sel)
        ks = jax.vmap(lambda kb, ib: kb[ib])(k, idx)
        vs = jax.vmap(lambda vb, ib: vb[ib])(v, idx)
        return attend_selected(qb, ks, vs, valid)

    out = lax.map(one_block, (blocks(q), blocks(iq), blocks(iw), pos_blocks))
    return jnp.moveaxis(out, 0, 1).reshape(b, s, N_HEADS * HEAD_DIM)


def sample_sparse_attention(q, k, v, iq, iw, ik, cache_k_l, cache_v_l, cache_ik_l, page_table):
    db, t = q.shape[:2]
    n_pages = page_table.shape[1]
    past = n_pages * PAGE_SIZE
    n_sel = min(TOPK_MAX, (past + t) // 4)
    ik_past = cache_ik_l[page_table].reshape(db, past, IDX_DIM)
    ik_all = jnp.concatenate([ik_past, ik.astype(ik_past.dtype)], axis=1)
    q_pos = past + jnp.arange(t, dtype=jnp.int32)
    sc = indexer_scores(iq, iw, ik_all)
    idx, valid = select_keys(sc, q_pos, n_sel)

    def one_query(args):
        q_t, idx_t, valid_t = args
        is_past = idx_t < past
        p_idx = jnp.minimum(idx_t, past - 1)
        phys = jnp.take_along_axis(page_table, p_idx // PAGE_SIZE, axis=1)
        off = p_idx % PAGE_SIZE
        n_idx = jnp.clip(idx_t - past, 0, t - 1)
        ks = jnp.where(is_past[..., None, None], cache_k_l[phys, off], gather_rows(k, n_idx))
        vs = jnp.where(is_past[..., None, None], cache_v_l[phys, off], gather_rows(v, n_idx))
        return attend_selected(q_t[:, None], ks[:, None], vs[:, None], valid_t[:, None])[:, 0]

    out = lax.map(one_query, (jnp.moveaxis(q, 1, 0), jnp.moveaxis(idx, 1, 0), jnp.moveaxis(valid, 1, 0)))
    return jnp.moveaxis(out, 0, 1)


def short_conv(u, state, conv_w, conv_b):
    t = u.shape[1]
    full = jnp.concatenate([state.astype(u.dtype), u], axis=1)
    y = conv_b + full[:, 0:t] * conv_w[0]
    for j in range(1, CONV_W):
        y = y + full[:, j:j + t] * conv_w[j]
    return y, full[:, t:]


def moe(x, w_router, router_bias, w_gate, w_up, w_down):
    b, t, d = x.shape
    xt = x.reshape(b * t, d)
    aff = jax.nn.sigmoid(jnp.einsum('nd,de->ne', xt.astype(jnp.float32), w_router.astype(jnp.float32)))
    biased = aff + router_bias.astype(jnp.float32)
    grp = biased.reshape(-1, N_GROUPS, EXPERTS_PER_GROUP)
    grp_score = lax.top_k(grp, TOP_K_EXPERTS)[0].sum(-1)
    best = jnp.argmax(grp_score, axis=-1)
    in_grp = (jnp.arange(N_EXPERTS) // EXPERTS_PER_GROUP)[None, :] == best[:, None]
    _, e_idx = lax.top_k(jnp.where(in_grp, biased, NEG_INF), TOP_K_EXPERTS)
    a_sel = jnp.take_along_axis(aff, e_idx, axis=1)
    g_sel = a_sel / a_sel.sum(-1, keepdims=True)
    gates = jnp.einsum('nk,nke->ne', g_sel, jax.nn.one_hot(e_idx, N_EXPERTS, dtype=jnp.float32))
    h = jax.nn.silu(jnp.einsum('nd,edf->nef', xt, w_gate)) * jnp.einsum('nd,edf->nef', xt, w_up)
    h = h * gates.astype(h.dtype)[:, :, None]
    y = jnp.einsum('nef,efd->nd', h, w_down)
    return y.reshape(b, t, d)


def decoder_layer(x, pos, conv_state, attn_fn, w_in, conv_w, conv_b, w_branch, w_out, ln1_g, ln1_b, ln2_g, ln2_b, w_router, router_bias, w_gate, w_up, w_down):
    b, t = x.shape[:2]
    z = jnp.einsum('btd,dn->btn', x, w_in)
    q, k, v, iq, ik, iw, ch, cb, cc, ga, gc = split_proj(z)
    q = rope(q.reshape(b, t, N_HEADS, HEAD_DIM), pos)
    k = rope(k.reshape(b, t, N_KV_HEADS, HEAD_DIM), pos)
    v = v.reshape(b, t, N_KV_HEADS, HEAD_DIM)
    iq = rope(iq.reshape(b, t, N_IDX_HEADS, IDX_DIM), pos)
    ik = rope(ik[:, :, None, :], pos)[:, :, 0]
    iw = iw * N_IDX_HEADS ** -0.5
    a = attn_fn(q, k, v, iq, iw, ik)
    cy, new_conv = short_conv(cc * ch, conv_state, conv_w, conv_b)
    c = cb * cy
    a_d = jnp.einsum('btc,cd->btd', a, w_branch[0])
    c_d = jnp.einsum('btc,cd->btd', c, w_branch[1])
    mix = jnp.einsum('btd,de->bte', jax.nn.sigmoid(ga) * a_d + jax.nn.sigmoid(gc) * c_d, w_out)
    x = layer_norm(DEEPNORM_ALPHA * x + mix, ln1_g, ln1_b)
    x = layer_norm(DEEPNORM_ALPHA * x + moe(x, w_router, router_bias, w_gate, w_up, w_down), ln2_g, ln2_b)
    return x, k, v, ik, new_conv


def setup_inputs(seed: int = 0) -> dict:
    key = jax.random.key(seed)
    ks = jax.random.split(key, 24)
    f32 = jnp.float32
    n_pages = PAST_LEN // PAGE_SIZE
    n_used = DEC_BATCH * n_pages
    n_pool = n_used + max(1, n_used // 4)

    def nrm(k, shape, scale=1.0):
        return jax.random.normal(k, shape, f32) * scale

    page_table = jax.random.permutation(ks[6], n_pool)[:n_used].reshape(DEC_BATCH, n_pages).astype(jnp.int32)
    return {
        'x_prompt': nrm(ks[0], (BATCH, SEQ, D_MODEL)),
        'x_sample': nrm(ks[1], (DEC_BATCH, DEC_SEQ, D_MODEL)),
        'cache_k': nrm(ks[2], (DEPTH, n_pool, PAGE_SIZE, N_KV_HEADS, HEAD_DIM)),
        'cache_v': nrm(ks[3], (DEPTH, n_pool, PAGE_SIZE, N_KV_HEADS, HEAD_DIM)),
        'cache_ik': nrm(ks[4], (DEPTH, n_pool, PAGE_SIZE, IDX_DIM)),
        'state_conv': nrm(ks[5], (DEPTH, DEC_BATCH, CONV_W - 1, CONV_DIM)),
        'page_table': page_table,
        'w_in': nrm(ks[7], (DEPTH, D_MODEL, IN_WIDTH), D_MODEL ** -0.5),
        'conv_w': nrm(ks[8], (DEPTH, CONV_W, CONV_DIM), CONV_W ** -0.5),
        'conv_b': nrm(ks[9], (DEPTH, CONV_DIM), 0.01),
        'w_branch': nrm(ks[10], (DEPTH, N_BRANCHES, BRANCH_DIM, D_MODEL), BRANCH_DIM ** -0.5 * DEEPNORM_BETA),
        'w_out': nrm(ks[11], (DEPTH, D_MODEL, D_MODEL), D_MODEL ** -0.5 * DEEPNORM_BETA),
        'ln1_g': 1.0 + nrm(ks[12], (DEPTH, D_MODEL), 0.01),
        'ln1_b': nrm(ks[13], (DEPTH, D_MODEL), 0.01),
        'ln2_g': 1.0 + nrm(ks[14], (DEPTH, D_MODEL), 0.01),
        'ln2_b': nrm(ks[15], (DEPTH, D_MODEL), 0.01),
        'w_router': nrm(ks[16], (D_MODEL, N_EXPERTS), D_MODEL ** -0.5),
        'router_bias': nrm(ks[17], (N_EXPERTS,), 0.01),
        'w_exp_gate': nrm(ks[18], (DEPTH, N_EXPERTS, D_MODEL, D_EXPERT), D_MODEL ** -0.5),
        'w_exp_up': nrm(ks[19], (DEPTH, N_EXPERTS, D_MODEL, D_EXPERT), D_MODEL ** -0.5),
        'w_exp_down': nrm(ks[20], (DEPTH, N_EXPERTS, D_EXPERT, D_MODEL), D_EXPERT ** -0.5 * DEEPNORM_BETA),
    }


def reference(x_prompt, x_sample, cache_k, cache_v, cache_ik, state_conv, page_table, w_in, conv_w, conv_b, w_branch, w_out, ln1_g, ln1_b, ln2_g, ln2_b, w_router, router_bias, w_exp_gate, w_exp_up, w_exp_down):
    n_past = page_table.shape[1] * PAGE_SIZE
    pos_prompt = jnp.arange(x_prompt.shape[1], dtype=jnp.int32)
    pos_sample = n_past + jnp.arange(x_sample.shape[1], dtype=jnp.int32)
    zero_conv = jnp.zeros((x_prompt.shape[0], CONV_W - 1, CONV_DIM), x_prompt.dtype)
    xp, xs = x_prompt, x_sample
    kp, vp, ikp, cp = [], [], [], []
    ksm, vsm, iksm, csm = [], [], [], []
    for l in range(DEPTH):
        shared = (w_in[l], conv_w[l], conv_b[l], w_branch[l], w_out[l], ln1_g[l], ln1_b[l], ln2_g[l], ln2_b[l], w_router, router_bias, w_exp_gate[l], w_exp_up[l], w_exp_down[l])
        xp, k_p, v_p, ik_p, c_p = decoder_layer(xp, pos_prompt, zero_conv, prompt_sparse_attention, *shared)
        attn_s = functools.partial(sample_sparse_attention, cache_k_l=cache_k[l], cache_v_l=cache_v[l], cache_ik_l=cache_ik[l], page_table=page_table)
        xs, k_s, v_s, ik_s, c_s = decoder_layer(xs, pos_sample, state_conv[l], attn_s, *shared)
        kp.append(k_p); vp.append(v_p); ikp.append(ik_p); cp.append(c_p)
        ksm.append(k_s); vsm.append(v_s); iksm.append(ik_s); csm.append(c_s)
    return (xp, xs, jnp.stack(kp), jnp.stack(vp), jnp.stack(ikp), jnp.stack(cp), jnp.stack(ksm), jnp.stack(vsm), jnp.stack(iksm), jnp.stack(csm))
```

```python
import functools

import numpy as np
import jax
import jax.numpy as jnp
from jax import lax
from jax.experimental import pallas as pl
from jax.experimental.pallas import tpu as pltpu

N_HEADS = 8
HEAD_DIM = 64
N_KV_HEADS = 2
KV_GROUP = N_HEADS // N_KV_HEADS
N_IDX_HEADS = 4
IDX_DIM = 64
TOPK_MAX = 256
ROPE_THETA = 10000.0
CONV_DIM = 512
CONV_W = 3
N_EXPERTS = 16
N_GROUPS = 4
EXPERTS_PER_GROUP = N_EXPERTS // N_GROUPS
TOP_K_EXPERTS = 2
D_EXPERT = 256
LN_EPS = 1e-5
NEG_INF = -1e30
SPLIT_SIZES = (N_HEADS * HEAD_DIM, N_KV_HEADS * HEAD_DIM, N_KV_HEADS * HEAD_DIM, N_IDX_HEADS * IDX_DIM,
               IDX_DIM, N_IDX_HEADS, CONV_DIM, CONV_DIM, CONV_DIM)

LANES = 128
SUBLANES = 8
VMEM_LIMIT = 56 * 1024 * 1024
F32 = jnp.float32
MXU_DTYPE = jnp.bfloat16
F32_MIN_NORMAL = float(np.finfo(np.float32).tiny)
HEAD_PERM = tuple(g * KV_GROUP + j for j in range(KV_GROUP) for g in range(N_KV_HEADS))

_NT = (((1,), (1,)), ((), ()))


def _dot(a, b):
    return jnp.dot(a, b, preferred_element_type=F32)


def _dot_nt(a, b):
    return lax.dot_general(a, b, _NT, preferred_element_type=F32)


def _params(n_grid):
    return pltpu.CompilerParams(dimension_semantics=("arbitrary",) * n_grid, vmem_limit_bytes=VMEM_LIMIT)


def _layer_norm(y, g, b):
    mu = jnp.mean(y, axis=-1, keepdims=True)
    yc = y - mu
    var = jnp.mean(yc * yc, axis=-1, keepdims=True)
    return yc * lax.rsqrt(var + LN_EPS) * g + b


def _attn_proj_kernel(x_ref, w_ref, cos_ref, sin_ref, q_ref, k_ref, v_ref, iq_ref, ik_ref, iw_ref,
                      kb_ref, vb_ref, ikb_ref):
    tm = x_ref.shape[0]
    xb = x_ref[...].astype(MXU_DTYPE)
    cos = cos_ref[...]
    sin = sin_ref[...]
    lane = lax.broadcasted_iota(jnp.int32, (tm, LANES), 1)
    first_half = (lane & (HEAD_DIM // 2)) == 0

    def proj(col):
        return _dot(xb, w_ref[:, col:col + LANES])

    def rope(g):
        partner = jnp.where(first_half, pltpu.roll(g, LANES - HEAD_DIM // 2, 1), pltpu.roll(g, HEAD_DIM // 2, 1))
        return g * cos + partner * sin

    for j in range(4):
        q_ref[:, LANES * j:LANES * (j + 1)] = (rope(proj(LANES * j)) * HEAD_DIM ** -0.5).astype(q_ref.dtype)
    k = rope(proj(512))
    k_ref[...] = k
    kb_ref[...] = k.astype(kb_ref.dtype)
    v = proj(640)
    v_ref[...] = v
    vb_ref[...] = v.astype(vb_ref.dtype)
    for j in range(2):
        iq_ref[:, LANES * j:LANES * (j + 1)] = rope(proj(768 + LANES * j)).astype(iq_ref.dtype)
    ik2 = rope(proj(1024))
    ik_ref[...] = ik2[:, :IDX_DIM]
    ikb_ref[...] = ik2.astype(ikb_ref.dtype)
    iw_ref[...] = proj(1152) * (N_IDX_HEADS ** -0.5 * IDX_DIM ** -0.5)


def _attn_proj(x, wa, layer, cos, sin, tm, low_dtype):
    n, d = x.shape
    n_pos = cos.shape[0]
    nblk_pos = n_pos // tm
    row = lambda i: (i, 0)
    pos = lambda i: (i % nblk_pos, 0)
    out_shape = [
        jax.ShapeDtypeStruct((n, 512), low_dtype),
        jax.ShapeDtypeStruct((n, 128), F32),
        jax.ShapeDtypeStruct((n, 128), F32),
        jax.ShapeDtypeStruct((n, 256), low_dtype),
        jax.ShapeDtypeStruct((n, IDX_DIM), F32),
        jax.ShapeDtypeStruct((n, 128), F32),
        jax.ShapeDtypeStruct((n, 128), MXU_DTYPE),
        jax.ShapeDtypeStruct((n, 128), MXU_DTYPE),
        jax.ShapeDtypeStruct((n, 128), low_dtype),
    ]
    widths = (512, 128, 128, 256, IDX_DIM, 128, 128, 128, 128)
    return pl.pallas_call(
        _attn_proj_kernel,
        grid=(n // tm,),
        in_specs=[
            pl.BlockSpec((tm, d), row),
            pl.BlockSpec((None, d, wa.shape[2]), lambda i: (layer, 0, 0)),
            pl.BlockSpec((tm, LANES), pos),
            pl.BlockSpec((tm, LANES), pos),
        ],
        out_specs=[pl.BlockSpec((tm, w), row) for w in widths],
        out_shape=out_shape,
        compiler_params=_params(1),
        name="attn_proj",
    )(x, wa, cos, sin)


def _ordered_key_to_float(u):
    int_min = jnp.int32(-2 ** 31)
    ck = u ^ int_min
    bits = jnp.where(ck >= 0, ck, ck ^ jnp.int32(2 ** 31 - 1))
    return lax.bitcast_convert_type(bits, F32)


def _threshold_search(count_ge, rows, k_sel):
    def bit_body(it, carry):
        u, cnt_t = carry
        bit = lax.shift_left(jnp.int32(1), jnp.int32(31) - it)
        cu = u | bit
        cnt = count_ge(_ordered_key_to_float(cu))
        take = cnt >= k_sel
        return jnp.where(take, cu, u), jnp.where(take, cnt, cnt_t)

    u0 = jnp.zeros((rows, 1), jnp.int32)
    c0 = jnp.full((rows, 1), 3e38, F32)
    u, cnt_t = lax.fori_loop(0, 32, bit_body, (u0, c0))
    return _ordered_key_to_float(u), cnt_t


def _indexer_scores(dots, wcol, zval):
    s = None
    for d, w in zip(dots, wcol):
        t = jnp.maximum(d, 0.0) * w
        s = t if s is None else s + t
    return jnp.where(s == 0.0, zval, s)


def _zero_value(wcol):
    wmax = functools.reduce(jnp.maximum, wcol)
    return jnp.where(wmax < 0.0, -F32_MIN_NORMAL, 0.0).astype(F32)


def _prompt_attn_kernel(q_ref, iq_ref, iw_ref, kb_ref, vb_ref, ikb_ref, tri_ref, a_ref, s_scr, acc_scr, *, k_sel):
    qb = q_ref.shape[0]
    i = pl.program_id(1)
    lane = lax.broadcasted_iota(jnp.int32, (qb, LANES), 1)
    lo_half = lane < HEAD_DIM
    halves = (lo_half, jnp.logical_not(lo_half))
    iw = iw_ref[...]
    wcol = [iw[:, h:h + 1] for h in range(N_IDX_HEADS)]
    zval = _zero_value(wcol)
    rows = lax.broadcasted_iota(jnp.int32, (qb, qb), 0)
    cols = lax.broadcasted_iota(jnp.int32, (qb, qb), 1)
    causal = cols <= rows
    zero_iq = jnp.zeros((), iq_ref.dtype)
    iqp = [jnp.where(halves[h % 2], iq_ref[:, LANES * (h // 2):LANES * (h // 2 + 1)], zero_iq)
           for h in range(N_IDX_HEADS)]

    def key_rows(ref, c):
        return ref[pl.ds(pl.multiple_of(c * qb, qb), qb), :]

    def score_chunk(c, diag):
        ikc = key_rows(ikb_ref, c)
        s = _indexer_scores([_dot_nt(iqp[h], ikc) for h in range(N_IDX_HEADS)], wcol, zval)
        if diag:
            s = jnp.where(causal, s, NEG_INF)
        s_scr[c] = s

    def score_body(c, carry):
        score_chunk(c, False)
        return carry

    lax.fori_loop(0, i, score_body, 0)
    score_chunk(i, True)

    def count(pred):
        def body(c, acc):
            ones = jnp.where(pred(s_scr[c]), 1.0, 0.0)
            part = ones[:, :LANES]
            for t in range(1, qb // LANES):
                part = part + ones[:, LANES * t:LANES * (t + 1)]
            return acc + part

        acc = lax.fori_loop(0, i + 1, body, jnp.zeros((qb, LANES), F32))
        return jnp.sum(acc, axis=1, keepdims=True)

    thr, cnt_ge = _threshold_search(lambda t: count(lambda s: s >= t), qb, float(k_sel))
    qpos = i * qb + lax.broadcasted_iota(jnp.int32, (qb, 1), 0)
    short = qpos < k_sel
    thr = jnp.where(short, -jnp.inf, thr)
    need = float(k_sel) - count(lambda s: s > thr)
    has_tie = jnp.logical_and(jnp.logical_not(short), cnt_ge > float(k_sel))
    any_tie = jnp.max(jnp.where(has_tie, 1.0, 0.0)) > 0.5

    def attn_chunk(c, carry, diag, ties):
        m_l, l_l, tie_seen = carry
        s = s_scr[c]
        sel = s > thr
        if ties:
            eq = s == thr
            eq_f = jnp.where(eq, 1.0, 0.0)
            before = _dot(eq_f.astype(tri_ref.dtype), tri_ref[...])
            sel = jnp.logical_or(sel, jnp.logical_and(eq, tie_seen + before < need))
            tie_seen = tie_seen + jnp.sum(eq_f, axis=1, keepdims=True)
        if diag:
            sel = jnp.logical_and(sel, causal)
        kc = key_rows(kb_ref, c)
        vc = key_rows(vb_ref, c)
        zero_q = jnp.zeros((), q_ref.dtype)
        m_n, l_n = [], []
        for hh in range(N_HEADS):
            g, j = divmod(hh, KV_GROUP)
            qpad = jnp.where(halves[g], q_ref[:, LANES * j:LANES * (j + 1)], zero_q)
            lg = jnp.where(sel, _dot_nt(qpad, kc), NEG_INF)
            mx = jnp.maximum(m_l[hh], jnp.max(lg, axis=1, keepdims=True))
            p = jnp.exp(lg - mx)
            alpha = jnp.exp(m_l[hh] - mx)
            l_n.append(l_l[hh] * alpha + jnp.sum(p, axis=1, keepdims=True))
            m_n.append(mx)
            acc_scr[hh] = acc_scr[hh] * alpha + _dot(p.astype(vc.dtype), vc)
        return tuple(m_n), tuple(l_n), tie_seen

    def run(ties):
        acc_scr[...] = jnp.zeros_like(acc_scr)
        init = (tuple(jnp.full((qb, 1), NEG_INF, F32) for _ in range(N_HEADS)),
                tuple(jnp.zeros((qb, 1), F32) for _ in range(N_HEADS)),
                jnp.zeros((qb, 1), F32))
        carry = lax.fori_loop(0, i, lambda c, cr: attn_chunk(c, cr, False, ties), init)
        _, l_l, _ = attn_chunk(i, carry, True, ties)
        for j in range(KV_GROUP):
            o0 = acc_scr[j] / l_l[j]
            o1 = acc_scr[KV_GROUP + j] / l_l[KV_GROUP + j]
            a_ref[:, LANES * j:LANES * (j + 1)] = jnp.where(lo_half, o0, o1).astype(a_ref.dtype)

    @pl.when(any_tie)
    def _():
        run(True)

    @pl.when(jnp.logical_not(any_tie))
    def _():
        run(False)


def _prompt_attn(q, iq, iw, kb, vb, ikb, tri, batch, seq, qb, k_sel):
    n = q.shape[0]
    nq = seq // qb
    qrow = lambda b, i: (b * nq + i, 0)
    keys = lambda b, i: (b, 0)
    return pl.pallas_call(
        functools.partial(_prompt_attn_kernel, k_sel=k_sel),
        grid=(batch, nq),
        in_specs=[
            pl.BlockSpec((qb, 512), qrow),
            pl.BlockSpec((qb, 256), qrow),
            pl.BlockSpec((qb, LANES), qrow),
            pl.BlockSpec((seq, LANES), keys),
            pl.BlockSpec((seq, LANES), keys),
            pl.BlockSpec((seq, LANES), keys),
            pl.BlockSpec((qb, qb), lambda b, i: (0, 0)),
        ],
        out_specs=pl.BlockSpec((qb, 512), qrow),
        out_shape=jax.ShapeDtypeStruct((n, 512), MXU_DTYPE),
        scratch_shapes=[pltpu.VMEM((nq, qb, qb), F32), pltpu.VMEM((N_HEADS, qb, LANES), F32)],
        compiler_params=_params(2),
        name="prompt_attn",
    )(q, iq, iw, kb, vb, ikb, tri)


def _sample_attn_kernel(pt_ref, q_ref, iq_ref, iw_ref, k_ref, v_ref, ik_ref, tri_ref, ck_ref, cv_ref, cik_ref,
                        a_ref, kt_buf, vt_buf, ikt_buf, sems, *, layer, rps, t_new, n_pages, page, k_sel):
    step = pl.program_id(0)
    n_steps = pl.num_programs(0)
    slot = step % 2
    past = n_pages * page
    n_tile = past // LANES

    def page_copies(st, sl):
        copies = []
        for r in range(rps):
            for p in range(n_pages):
                phys = pt_ref[st * rps + r, p]
                dst = pl.ds(p * page, page)
                copies.append(pltpu.make_async_copy(ck_ref.at[layer, phys], kt_buf.at[sl, r, :, :, dst], sems.at[sl, 0]))
                copies.append(pltpu.make_async_copy(cv_ref.at[layer, phys], vt_buf.at[sl, r, :, :, dst], sems.at[sl, 1]))
                copies.append(pltpu.make_async_copy(cik_ref.at[layer, phys], ikt_buf.at[sl, r, :, dst], sems.at[sl, 2]))
        return copies

    @pl.when(step == 0)
    def _():
        for cp in page_copies(0, 0):
            cp.start()

    @pl.when(step + 1 < n_steps)
    def _():
        for cp in page_copies(step + 1, 1 - slot):
            cp.start()

    for cp in page_copies(step, slot):
        cp.wait()

    trow = lax.broadcasted_iota(jnp.int32, (t_new, LANES), 0)
    tcol = lax.broadcasted_iota(jnp.int32, (t_new, LANES), 1)
    causal_new = tcol <= trow
    tri = tri_ref[...]
    qpos = past + lax.broadcasted_iota(jnp.int32, (t_new, 1), 0)
    short = qpos < k_sel

    def pad_keys(x):
        return jnp.concatenate([x, jnp.zeros((LANES - t_new, x.shape[1]), x.dtype)], axis=0).astype(MXU_DTYPE)

    for r in range(rps):
        rows = slice(r * t_new, (r + 1) * t_new)
        iq = iq_ref[rows, :]
        iq4 = jnp.concatenate([iq[:, IDX_DIM * h:IDX_DIM * (h + 1)] for h in range(N_IDX_HEADS)], axis=0)
        iq4 = iq4.astype(MXU_DTYPE)
        iw = iw_ref[rows, :]
        wcol = [iw[:, h:h + 1] for h in range(N_IDX_HEADS)]
        zval = _zero_value(wcol)
        d_past = _dot(iq4, ikt_buf[slot, r].astype(MXU_DTYPE))
        d_new = _dot_nt(iq4, pad_keys(ik_ref[rows, :]))
        split = lambda d: [d[t_new * h:t_new * (h + 1)] for h in range(N_IDX_HEADS)]
        s_past = _indexer_scores(split(d_past), wcol, zval)
        s_new = jnp.where(causal_new, _indexer_scores(split(d_new), wcol, zval), NEG_INF)

        def count(pred):
            return (jnp.sum(jnp.where(pred(s_past), 1.0, 0.0), axis=1, keepdims=True)
                    + jnp.sum(jnp.where(pred(s_new), 1.0, 0.0), axis=1, keepdims=True))

        thr, _ = _threshold_search(lambda t: count(lambda s: s >= t), t_new, float(k_sel))
        thr = jnp.where(short, -jnp.inf, thr)
        need = float(k_sel) - count(lambda s: s > thr)

        eq_past = jnp.where(s_past == thr, 1.0, 0.0)
        eq_tiles = [eq_past[:, LANES * c:LANES * (c + 1)] for c in range(n_tile)]
        before = _dot(jnp.concatenate(eq_tiles, axis=0).astype(tri.dtype), tri)
        seen = jnp.zeros((t_new, 1), F32)
        sel_tiles = []
        for c in range(n_tile):
            sc = s_past[:, LANES * c:LANES * (c + 1)]
            rank = seen + before[t_new * c:t_new * (c + 1)]
            sel_tiles.append(jnp.where(jnp.logical_or(sc > thr, jnp.logical_and(sc == thr, rank < need)), 0.0, NEG_INF))
            seen = seen + jnp.sum(eq_tiles[c], axis=1, keepdims=True)
        bias_past = jnp.concatenate(sel_tiles, axis=1)
        eq_new = jnp.where(s_new == thr, 1.0, 0.0)
        rank_new = seen + _dot(eq_new.astype(tri.dtype), tri)
        sel_new = jnp.logical_or(s_new > thr, jnp.logical_and(s_new == thr, rank_new < need))
        sel_new = jnp.logical_and(sel_new, causal_new)

        q = q_ref[rows, :]
        for g in range(N_KV_HEADS):
            lo = HEAD_DIM * g
            qg = jnp.concatenate([q[:, LANES * j + lo:LANES * j + lo + HEAD_DIM] for j in range(KV_GROUP)], axis=0)
            qg = qg.astype(MXU_DTYPE)
            lp = _dot(qg, kt_buf[slot, r, g].astype(MXU_DTYPE))
            ln = _dot_nt(qg, pad_keys(k_ref[rows, lo:lo + HEAD_DIM]))
            vt = vt_buf[slot, r, g].astype(MXU_DTYPE)
            vn = pad_keys(v_ref[rows, lo:lo + HEAD_DIM])
            for j in range(KV_GROUP):
                hs = slice(t_new * j, t_new * (j + 1))
                lpj = jnp.where(bias_past == 0.0, lp[hs], NEG_INF)
                lnj = jnp.where(sel_new, ln[hs], NEG_INF)
                mx = jnp.maximum(jnp.max(lpj, axis=1, keepdims=True), jnp.max(lnj, axis=1, keepdims=True))
                pp = jnp.exp(lpj - mx)
                pn = jnp.exp(lnj - mx)
                den = jnp.sum(pp, axis=1, keepdims=True) + jnp.sum(pn, axis=1, keepdims=True)
                o = _dot_nt(pp.astype(MXU_DTYPE), vt) + _dot(pn.astype(MXU_DTYPE), vn)
                a_ref[rows, LANES * j + lo:LANES * j + lo + HEAD_DIM] = (o / den).astype(a_ref.dtype)


def _sample_attn(page_table, q, iq, iw, k, v, ik, tri, ck_t, cv_t, cik_t, layer, rps, t_new, k_sel):
    n = q.shape[0]
    n_req, n_pages = page_table.shape
    page = ck_t.shape[-1]
    past = n_pages * page
    rows = rps * t_new
    blk = lambda w: pl.BlockSpec((rows, w), lambda s, pt: (s, 0))
    any_spec = pl.BlockSpec(memory_space=pl.ANY)
    return pl.pallas_call(
        functools.partial(_sample_attn_kernel, layer=layer, rps=rps, t_new=t_new, n_pages=n_pages, page=page,
                          k_sel=k_sel),
        grid_spec=pltpu.PrefetchScalarGridSpec(
            num_scalar_prefetch=1,
            grid=(n_req // rps,),
            in_specs=[blk(512), blk(256), blk(LANES), blk(LANES), blk(LANES), blk(IDX_DIM),
                      pl.BlockSpec((LANES, LANES), lambda s, pt: (0, 0)),
                      any_spec, any_spec, any_spec],
            out_specs=blk(512),
            scratch_shapes=[
                pltpu.VMEM((2, rps, N_KV_HEADS, HEAD_DIM, past), F32),
                pltpu.VMEM((2, rps, N_KV_HEADS, HEAD_DIM, past), F32),
                pltpu.VMEM((2, rps, IDX_DIM, past), F32),
                pltpu.SemaphoreType.DMA((2, 3)),
            ],
        ),
        out_shape=jax.ShapeDtypeStruct((n, 512), F32),
        compiler_params=_params(1),
        name="sample_attn",
    )(page_table, q, iq, iw, k, v, ik, tri, ck_t, cv_t, cik_t)


def _router_gates(x1, wr_ref, rb_ref):
    tm = x1.shape[0]
    logits = jnp.dot(x1, wr_ref[...], preferred_element_type=F32, precision=lax.Precision.HIGHEST)
    aff = jax.nn.sigmoid(logits)
    biased = aff + rb_ref[...]
    lane = lax.broadcasted_iota(jnp.int32, (tm, LANES), 1)
    member = lane % EXPERTS_PER_GROUP
    group = lane // EXPERTS_PER_GROUP

    def other_member(x, d):
        return jnp.where(member + d < EXPERTS_PER_GROUP, pltpu.roll(x, LANES - d, 1),
                         pltpu.roll(x, EXPERTS_PER_GROUP - d, 1))

    def other_group(x, d):
        sh = EXPERTS_PER_GROUP * d
        return jnp.where(group + d < N_GROUPS, pltpu.roll(x, LANES - sh, 1), pltpu.roll(x, N_EXPERTS - sh, 1))

    def beaten_count(x, other, pos, n):
        cnt = jnp.zeros(x.shape, F32)
        for d in range(1, n):
            o = other(x, d)
            wins = jnp.logical_or(o > x, jnp.logical_and(o == x, pos + d >= n))
            cnt = cnt + jnp.where(wins, 1.0, 0.0)
        return cnt

    def group_sum(x):
        tot = x
        for d in range(1, EXPERTS_PER_GROUP):
            tot = tot + other_member(x, d)
        return tot

    top2 = beaten_count(biased, other_member, member, EXPERTS_PER_GROUP) < float(TOP_K_EXPERTS)
    grp_score = group_sum(jnp.where(top2, biased, 0.0))
    best = beaten_count(grp_score, other_group, group, N_GROUPS) < 1.0
    sel = jnp.logical_and(jnp.logical_and(top2, best), lane < N_EXPERTS)
    den = group_sum(jnp.where(sel, aff, 0.0))
    return jnp.where(sel, aff / jnp.where(sel, den, 1.0), 0.0)


def _mix_kernel(*refs, alpha, sample, t_new, blocks_per_seq):
    if sample:
        (x_ref, a_ref, wm_ref, cw_ref, cbias_ref, wb0_ref, wb1_ref, wo_ref, g_ref, b_ref, wr_ref, rb_ref,
         sp1_ref, sp2_ref, x1_ref, gates_ref, u_ref, ubuf) = refs
    else:
        (x_ref, a_ref, wm_ref, cw_ref, cbias_ref, wb0_ref, wb1_ref, wo_ref, g_ref, b_ref, wr_ref, rb_ref,
         x1_ref, gates_ref, u_ref, ubuf) = refs
    tm = x_ref.shape[0]
    hist = SUBLANES
    x = x_ref[...]
    xb = x.astype(MXU_DTYPE)
    c0, c1, c2, c3, c4 = 0, CONV_DIM, 2 * CONV_DIM, 3 * CONV_DIM, 3 * CONV_DIM + x.shape[1]
    ch = _dot(xb, wm_ref[:, c0:c1])
    cc = _dot(xb, wm_ref[:, c2:c3])
    u = cc * ch
    if sample:
        ubuf[0:hist, :] = jnp.zeros((hist, CONV_DIM), F32)
    else:
        @pl.when(pl.program_id(0) % blocks_per_seq == 0)
        def _():
            ubuf[0:hist, :] = jnp.zeros((hist, CONV_DIM), F32)
    ubuf[hist:hist + tm, :] = u
    prev1 = ubuf[hist - 1:hist - 1 + tm, :]
    prev2 = ubuf[hist - 2:hist - 2 + tm, :]
    if sample:
        t = lax.broadcasted_iota(jnp.int32, (tm, 1), 0) % t_new
        prev1 = jnp.where(t >= 1, prev1, sp1_ref[...])
        prev2 = jnp.where(t >= 2, prev2, sp2_ref[...])
        u_ref[...] = u
    else:
        u_ref[0] = u[tm - hist:tm, :]
        ubuf[0:hist, :] = u[tm - hist:tm, :]
    cw = cw_ref[...]
    cy = cbias_ref[...] + prev2 * cw[0:1, :]
    cy = cy + prev1 * cw[1:2, :]
    cy = cy + u * cw[2:3, :]
    cb = _dot(xb, wm_ref[:, c1:c2])
    c = cb * cy
    a_d = _dot(a_ref[...].astype(MXU_DTYPE), wb0_ref[...])
    c_d = _dot(c.astype(MXU_DTYPE), wb1_ref[...])
    ga = _dot(xb, wm_ref[:, c3:c4])
    merged = jax.nn.sigmoid(ga) * a_d
    gc = _dot(xb, wm_ref[:, c4:])
    merged = merged + jax.nn.sigmoid(gc) * c_d
    mix = _dot(merged.astype(MXU_DTYPE), wo_ref[...])
    x1 = _layer_norm(alpha * x + mix, g_ref[...], b_ref[...])
    x1_ref[...] = x1
    gates_ref[...] = _router_gates(x1, wr_ref, rb_ref)


def _const_spec(shape, index):
    return pl.BlockSpec(shape, index, pipeline_mode=pl.Buffered(1))


def _mix(x, a, wm, conv_w, conv_b, wb0, wb1, wo, ln_g, ln_b, wr, rb, layer, tm, alpha, seq=None, sp1=None, sp2=None,
         t_new=None):
    n, d = x.shape
    sample = sp1 is not None
    row = lambda i: (i, 0)
    lay3 = lambda i: (layer, 0, 0)
    in_specs = [
        pl.BlockSpec((tm, d), row),
        pl.BlockSpec((tm, 512), row),
        _const_spec((None,) + wm.shape[1:], lay3),
        pl.BlockSpec((None, CONV_W, CONV_DIM), lay3),
        pl.BlockSpec((None, 1, CONV_DIM), lay3),
        _const_spec((None,) + wb0.shape[1:], lay3),
        _const_spec((None,) + wb1.shape[1:], lay3),
        _const_spec((None,) + wo.shape[1:], lay3),
        pl.BlockSpec((None, 1, d), lay3),
        pl.BlockSpec((None, 1, d), lay3),
        pl.BlockSpec(wr.shape, lambda i: (0, 0)),
        pl.BlockSpec(rb.shape, lambda i: (0, 0)),
    ]
    args = [x, a, wm, conv_w, conv_b, wb0, wb1, wo, ln_g, ln_b, wr, rb]
    if sample:
        in_specs += [pl.BlockSpec((None, tm, CONV_DIM), lambda i: (layer, i, 0))] * 2
        args += [sp1, sp2]
        u_shape = jax.ShapeDtypeStruct((n, CONV_DIM), F32)
        u_spec = pl.BlockSpec((tm, CONV_DIM), row)
        blocks_per_seq = 1
    else:
        blocks_per_seq = seq // tm
        u_shape = jax.ShapeDtypeStruct((n // seq, SUBLANES, CONV_DIM), F32)
        u_spec = pl.BlockSpec((1, SUBLANES, CONV_DIM), lambda i: (i // blocks_per_seq, 0, 0))
    return pl.pallas_call(
        functools.partial(_mix_kernel, alpha=alpha, sample=sample, t_new=t_new, blocks_per_seq=blocks_per_seq),
        grid=(n // tm,),
        in_specs=in_specs,
        out_specs=[pl.BlockSpec((tm, d), row), pl.BlockSpec((tm, LANES), row), u_spec],
        out_shape=[jax.ShapeDtypeStruct((n, d), F32), jax.ShapeDtypeStruct((n, LANES), F32), u_shape],
        scratch_shapes=[pltpu.VMEM((SUBLANES + tm, CONV_DIM), F32)],
        compiler_params=_params(1),
        name="mix_sample" if sample else "mix_prompt",
    )(*args)


def _moe_kernel(x_ref, gates_ref, wg_ref, wu_ref, wd_ref, g_ref, b_ref, o_ref, acc_ref, *, alpha):
    x1 = x_ref[...]
    xb = x1.astype(MXU_DTYPE)
    gates = gates_ref[...]
    for e in range(N_EXPERTS):
        h = jax.nn.silu(_dot(xb, wg_ref[e])) * _dot(xb, wu_ref[e])
        h = h * gates[:, e:e + 1]
        y = _dot(h.astype(MXU_DTYPE), wd_ref[e])
        if e == 0:
            acc_ref[...] = y
        else:
            acc_ref[...] += y
    o_ref[...] = _layer_norm(alpha * x1 + acc_ref[...], g_ref[...], b_ref[...])


def _moe(x1, gates, wg, wu, wd, ln_g, ln_b, layer, tm, alpha):
    n, d = x1.shape
    row = lambda i: (i, 0)
    lay4 = lambda i: (layer, 0, 0, 0)
    lay3 = lambda i: (layer, 0, 0)
    return pl.pallas_call(
        functools.partial(_moe_kernel, alpha=alpha),
        grid=(n // tm,),
        in_specs=[
            pl.BlockSpec((tm, d), row),
            pl.BlockSpec((tm, LANES), row),
            _const_spec((None,) + wg.shape[1:], lay4),
            _const_spec((None,) + wu.shape[1:], lay4),
            _const_spec((None,) + wd.shape[1:], lay4),
            pl.BlockSpec((None, 1, d), lay3),
            pl.BlockSpec((None, 1, d), lay3),
        ],
        out_specs=pl.BlockSpec((tm, d), row),
        out_shape=jax.ShapeDtypeStruct((n, d), F32),
        scratch_shapes=[pltpu.VMEM((tm, d), F32)],
        compiler_params=_params(1),
        name="moe",
    )(x1, gates, wg, wu, wd, ln_g, ln_b)


def _rope_tables(pos):
    half = HEAD_DIM // 2
    inv = 1.0 / (ROPE_THETA ** (jnp.arange(half, dtype=F32) / half))
    ang = pos.astype(F32)[:, None] * inv[None, :]
    cos = jnp.cos(ang)
    sin = jnp.sin(ang)
    reps = LANES // HEAD_DIM
    return jnp.tile(cos, (1, 2 * reps)), jnp.tile(jnp.concatenate([-sin, sin], axis=1), (1, reps))


def _block(n, target):
    b = min(n, target)
    assert n % b == 0, (n, target)
    return b


def kernel(x_prompt, x_sample, cache_k, cache_v, cache_ik, state_conv, page_table, w_in, conv_w, conv_b, w_branch, w_out, ln1_g, ln1_b, ln2_g, ln2_b, w_router, router_bias, w_exp_gate, w_exp_up, w_exp_down):
    depth, d_model, _ = w_in.shape
    batch, seq, _ = x_prompt.shape
    n_req, t_new, _ = x_sample.shape
    n_pages = page_table.shape[1]
    page = cache_k.shape[2]
    past = n_pages * page
    assert t_new >= CONV_W - 1 and t_new % SUBLANES == 0 and page % LANES == 0
    alpha = float((2 * depth) ** 0.25)
    k_prompt_sel = min(TOPK_MAX, seq // 4)
    k_sample_sel = min(TOPK_MAX, (past + t_new) // 4)

    o = np.concatenate([[0], np.cumsum(SPLIT_SIZES)])
    perm = np.asarray(HEAD_PERM)
    wq = w_in[:, :, o[0]:o[1]].reshape(depth, d_model, N_HEADS, HEAD_DIM)[:, :, perm].reshape(depth, d_model, -1)
    w_ik = w_in[:, :, o[4]:o[5]]
    w_iw = jnp.pad(w_in[:, :, o[5]:o[6]], ((0, 0), (0, 0), (0, LANES - N_IDX_HEADS)))
    wa = jnp.concatenate([wq, w_in[:, :, o[1]:o[4]], w_ik, w_ik, w_iw], axis=-1).astype(MXU_DTYPE)
    wm = w_in[:, :, o[6]:].astype(MXU_DTYPE)
    wb0 = w_branch[:, 0].reshape(depth, N_HEADS, HEAD_DIM, d_model)[:, perm].reshape(depth, -1, d_model)
    wb0 = wb0.astype(MXU_DTYPE)
    wb1 = w_branch[:, 1].astype(MXU_DTYPE)
    wo = w_out.astype(MXU_DTYPE)
    wr = jnp.pad(w_router.astype(F32), ((0, 0), (0, LANES - N_EXPERTS)))
    rb = jnp.pad(router_bias.astype(F32), (0, LANES - N_EXPERTS)).reshape(1, LANES)
    wg = w_exp_gate.astype(MXU_DTYPE)
    wu = w_exp_up.astype(MXU_DTYPE)
    wd = w_exp_down.astype(MXU_DTYPE)
    ln1g, ln1b, ln2g, ln2b = (a.reshape(depth, 1, d_model) for a in (ln1_g, ln1_b, ln2_g, ln2_b))
    cbias = conv_b.reshape(depth, 1, CONV_DIM)

    cos_p, sin_p = _rope_tables(jnp.arange(seq, dtype=jnp.int32))
    pos_s = jnp.tile(past + jnp.arange(t_new, dtype=jnp.int32), n_req)
    cos_s, sin_s = _rope_tables(pos_s)
    ck_t = jnp.transpose(cache_k, (0, 1, 3, 4, 2))
    cv_t = jnp.transpose(cache_v, (0, 1, 3, 4, 2))
    cik_t = jnp.transpose(cache_ik, (0, 1, 3, 2))
    zeros_state = jnp.zeros((depth, n_req, t_new, CONV_DIM), F32)
    sp1 = zeros_state.at[:, :, 0].set(state_conv[:, :, 1]).reshape(depth, n_req * t_new, CONV_DIM)
    sp2 = zeros_state.at[:, :, :2].set(state_conv).reshape(depth, n_req * t_new, CONV_DIM)

    qb = _block(seq, 256)
    tri = (np.arange(qb)[:, None] < np.arange(qb)[None, :])
    tri_p = jnp.asarray(tri, MXU_DTYPE)
    tri_s = jnp.asarray(tri[:LANES, :LANES], MXU_DTYPE)
    tm_p = _block(seq, 512)
    n_s = n_req * t_new
    tm_s = _block(n_s, 512)
    rps = 2 if n_req % 2 == 0 else 1

    xp = x_prompt.reshape(batch * seq, d_model)
    xs = x_sample.reshape(n_s, d_model)
    outs = [[] for _ in range(8)]
    for l in range(depth):
        q, k, v, iq, ik, iw, kb, vb, ikb = _attn_proj(xp, wa, l, cos_p, sin_p, tm_p, MXU_DTYPE)
        a = _prompt_attn(q, iq, iw, kb, vb, ikb, tri_p, batch, seq, qb, k_prompt_sel)
        x1, gates, u_last = _mix(xp, a, wm, conv_w, cbias, wb0, wb1, wo, ln1g, ln1b, wr, rb, l, tm_p, alpha, seq=seq)
        xp = _moe(x1, gates, wg, wu, wd, ln2g, ln2b, l, tm_p, alpha)
        outs[0].append(k.reshape(batch, seq, N_KV_HEADS, HEAD_DIM))
        outs[1].append(v.reshape(batch, seq, N_KV_HEADS, HEAD_DIM))
        outs[2].append(ik.reshape(batch, seq, IDX_DIM))
        outs[3].append(u_last[:, SUBLANES - (CONV_W - 1):])

        q, k, v, iq, ik, iw, _, _, _ = _attn_proj(xs, wa, l, cos_s, sin_s, tm_s, F32)
        a = _sample_attn(page_table, q, iq, iw, k, v, ik, tri_s, ck_t, cv_t, cik_t, l, rps, t_new, k_sample_sel)
        x1, gates, u = _mix(xs, a, wm, conv_w, cbias, wb0, wb1, wo, ln1g, ln1b, wr, rb, l, tm_s, alpha,
                            sp1=sp1, sp2=sp2, t_new=t_new)
        xs = _moe(x1, gates, wg, wu, wd, ln2g, ln2b, l, tm_s, alpha)
        outs[4].append(k.reshape(n_req, t_new, N_KV_HEADS, HEAD_DIM))
        outs[5].append(v.reshape(n_req, t_new, N_KV_HEADS, HEAD_DIM))
        outs[6].append(ik.reshape(n_req, t_new, IDX_DIM))
        outs[7].append(u.reshape(n_req, t_new, CONV_DIM)[:, t_new - (CONV_W - 1):])
    stacked = [jnp.stack(o_) for o_ in outs]
    return (xp.reshape(batch, seq, d_model), xs.reshape(n_req, t_new, d_model), *stacked)
```

```python
import functools

import numpy as np
import jax
import jax.numpy as jnp
from jax import lax
from jax.experimental import pallas as pl
from jax.experimental.pallas import tpu as pltpu

N_HEADS = 8
HEAD_DIM = 64
N_KV_HEADS = 2
KV_GROUP = N_HEADS // N_KV_HEADS
N_IDX_HEADS = 4
IDX_DIM = 64
TOPK_MAX = 256
ROPE_THETA = 10000.0
CONV_DIM = 512
CONV_W = 3
N_EXPERTS = 16
N_GROUPS = 4
EXPERTS_PER_GROUP = N_EXPERTS // N_GROUPS
TOP_K_EXPERTS = 2
D_EXPERT = 256
LN_EPS = 1e-5
NEG_INF = -1e30
SPLIT_SIZES = (N_HEADS * HEAD_DIM, N_KV_HEADS * HEAD_DIM, N_KV_HEADS * HEAD_DIM, N_IDX_HEADS * IDX_DIM,
               IDX_DIM, N_IDX_HEADS, CONV_DIM, CONV_DIM, CONV_DIM)

LANES = 128
SUBLANES = 8
VMEM_LIMIT = 56 * 1024 * 1024
F32 = jnp.float32
MXU_DTYPE = jnp.bfloat16
F32_MIN_NORMAL = float(np.finfo(np.float32).tiny)
LOG2E = float(np.log2(np.e))
ROW_BLOCK = 64
COUNT_VREG_ROWS = 128
HEAD_PERM = tuple(g * KV_GROUP + j for j in range(KV_GROUP) for g in range(N_KV_HEADS))

_NT = (((1,), (1,)), ((), ()))


def _dot(a, b):
    return jnp.dot(a, b, preferred_element_type=F32)


def _dot_nt(a, b):
    return lax.dot_general(a, b, _NT, preferred_element_type=F32)


def _params(n_grid):
    return pltpu.CompilerParams(dimension_semantics=("arbitrary",) * n_grid, vmem_limit_bytes=VMEM_LIMIT)


def _layer_norm(y, g, b):
    mu = jnp.mean(y, axis=-1, keepdims=True)
    yc = y - mu
    var = jnp.mean(yc * yc, axis=-1, keepdims=True)
    return yc * lax.rsqrt(var + LN_EPS) * g + b


def _attn_proj_kernel(x_ref, w_ref, cos_ref, sin_ref, q_ref, k_ref, v_ref, iq_ref, ik_ref, iw_ref,
                      kb_ref, vb_ref, ikb_ref):
    tm = x_ref.shape[0]
    xb = x_ref[...].astype(MXU_DTYPE)
    cos = cos_ref[...]
    sin = sin_ref[...]
    lane = lax.broadcasted_iota(jnp.int32, (tm, LANES), 1)
    first_half = (lane & (HEAD_DIM // 2)) == 0

    def proj(col):
        return _dot(xb, w_ref[:, col:col + LANES])

    def rope(g):
        partner = jnp.where(first_half, pltpu.roll(g, LANES - HEAD_DIM // 2, 1), pltpu.roll(g, HEAD_DIM // 2, 1))
        return g * cos + partner * sin

    for j in range(4):
        q_ref[:, LANES * j:LANES * (j + 1)] = (rope(proj(LANES * j)) * (HEAD_DIM ** -0.5 * LOG2E)).astype(q_ref.dtype)
    k = rope(proj(512))
    k_ref[...] = k
    kb_ref[...] = k.astype(kb_ref.dtype)
    v = proj(640)
    v_ref[...] = v
    vb_ref[...] = v.astype(vb_ref.dtype)
    for j in range(2):
        iq_ref[:, LANES * j:LANES * (j + 1)] = rope(proj(768 + LANES * j)).astype(iq_ref.dtype)
    ik2 = rope(proj(1024))
    ik_ref[...] = ik2[:, :IDX_DIM]
    ikb_ref[...] = ik2.astype(ikb_ref.dtype)
    iw_ref[...] = proj(1152) * (N_IDX_HEADS ** -0.5 * IDX_DIM ** -0.5)


def _attn_proj(x, wa, layer, cos, sin, tm, low_dtype):
    n, d = x.shape
    n_pos = cos.shape[0]
    nblk_pos = n_pos // tm
    row = lambda i: (i, 0)
    pos = lambda i: (i % nblk_pos, 0)
    out_shape = [
        jax.ShapeDtypeStruct((n, 512), low_dtype),
        jax.ShapeDtypeStruct((n, 128), F32),
        jax.ShapeDtypeStruct((n, 128), F32),
        jax.ShapeDtypeStruct((n, 256), low_dtype),
        jax.ShapeDtypeStruct((n, IDX_DIM), F32),
        jax.ShapeDtypeStruct((n, 128), F32),
        jax.ShapeDtypeStruct((n, 128), MXU_DTYPE),
        jax.ShapeDtypeStruct((n, 128), MXU_DTYPE),
        jax.ShapeDtypeStruct((n, 128), low_dtype),
    ]
    widths = (512, 128, 128, 256, IDX_DIM, 128, 128, 128, 128)
    return pl.pallas_call(
        _attn_proj_kernel,
        grid=(n // tm,),
        in_specs=[
            pl.BlockSpec((tm, d), row),
            pl.BlockSpec((None, d, wa.shape[2]), lambda i: (layer, 0, 0)),
            pl.BlockSpec((tm, LANES), pos),
            pl.BlockSpec((tm, LANES), pos),
        ],
        out_specs=[pl.BlockSpec((tm, w), row) for w in widths],
        out_shape=out_shape,
        compiler_params=_params(1),
        name="attn_proj",
    )(x, wa, cos, sin)


def _ordered_key_to_float(u):
    int_min = jnp.int32(-2 ** 31)
    ck = u ^ int_min
    bits = jnp.where(ck >= 0, ck, ck ^ jnp.int32(2 ** 31 - 1))
    return lax.bitcast_convert_type(bits, F32)


def _bisect(count_ge, u, cnt_u, first_bit, n_bits, k_sel):
    def bit_body(it, carry):
        u, cnt_u = carry
        bit = lax.shift_left(jnp.int32(1), jnp.int32(first_bit) - it)
        cu = u | bit
        cnt = count_ge(_ordered_key_to_float(cu))
        take = cnt >= k_sel
        return jnp.where(take, cu, u), jnp.where(take, cnt, cnt_u)

    return lax.fori_loop(0, n_bits, bit_body, (u, cnt_u))


def _bisect_start(rows):
    return jnp.zeros((rows, 1), jnp.int32), jnp.full((rows, 1), 3e38, F32)


def _threshold_search(count_ge, rows, k_sel):
    u, cnt = _bisect(count_ge, *_bisect_start(rows), 31, 32, k_sel)
    return _ordered_key_to_float(u), cnt


def _floor_to_bf16(x):
    b = lax.bitcast_convert_type(x, jnp.int32)
    away = lax.shift_right_arithmetic(b, jnp.int32(31)) & jnp.int32(0xFFFF)
    return lax.bitcast_convert_type((b + away) & jnp.int32(-65536), F32).astype(jnp.bfloat16)


def _indexer_scores(dots, wcol, zval):
    s = None
    for d, w in zip(dots, wcol):
        t = jnp.maximum(d, 0.0) * w
        s = t if s is None else s + t
    return jnp.where(s == 0.0, zval, s)


def _zero_value(wcol):
    wmax = functools.reduce(jnp.maximum, wcol)
    return jnp.where(wmax < 0.0, -F32_MIN_NORMAL, 0.0).astype(F32)


def _prompt_attn_kernel(q_ref, iq_ref, iw_ref, kb_ref, vb_ref, ikb_ref, tri_ref, a_ref,
                        s_scr, hi_scr, qpad_scr, iqp_scr, w_scr, bias_scr, lg_scr, p_scr, alpha_scr, m_scr, acc_scr,
                        *, k_sel):
    qb = q_ref.shape[0]
    n_tile = qb // LANES
    i = pl.program_id(1)
    lane = lax.broadcasted_iota(jnp.int32, (qb, LANES), 1)
    row = lax.broadcasted_iota(jnp.int32, (qb, LANES), 0)
    lo_half = lane < HEAD_DIM
    halves = (lo_half, jnp.logical_not(lo_half))
    causal = [lane + LANES * t <= row for t in range(n_tile)]
    wide = lambda col: jnp.broadcast_to(col, (qb, LANES))
    tile = lambda t: slice(LANES * t, LANES * (t + 1))

    iw = iw_ref[...]
    wcol = [iw[:, h:h + 1] for h in range(N_IDX_HEADS)]
    zval = wide(_zero_value(wcol))
    for h in range(N_IDX_HEADS):
        w_scr[h] = wide(wcol[h])
        iqp_scr[h] = jnp.where(halves[h % 2], iq_ref[:, tile(h // 2)], jnp.zeros((), iq_ref.dtype))
    for hh in range(N_HEADS):
        g, j = divmod(hh, KV_GROUP)
        qpad_scr[hh] = jnp.where(halves[g], q_ref[:, tile(j)], jnp.zeros((), q_ref.dtype))

    def key_rows(ref, c):
        return ref[pl.ds(pl.multiple_of(c * qb, qb), qb), :]

    def score_chunk(c, diag):
        ikc = key_rows(ikb_ref, c)
        dots = [_dot_nt(iqp_scr[h], ikc) for h in range(N_IDX_HEADS)]
        for t in range(n_tile):
            s = None
            for h in range(N_IDX_HEADS):
                term = jnp.maximum(dots[h][:, tile(t)], 0.0) * w_scr[h]
                s = term if s is None else s + term
            s = jnp.where(s == 0.0, zval, s)
            if diag:
                s = jnp.where(causal[t], s, NEG_INF)
            s_scr[c, :, tile(t)] = s
            hi_scr[c, :, tile(t)] = _floor_to_bf16(s)

    def score_body(c, carry):
        score_chunk(c, False)
        return carry

    lax.fori_loop(0, i, score_body, 0)
    score_chunk(i, True)

    def count(scr, thr_b, one, zero, strict=False):
        group = min(qb, COUNT_VREG_ROWS * (4 // jnp.dtype(scr.dtype).itemsize))
        parts = []
        for r0 in range(0, qb, group):
            rs = slice(r0, r0 + group)
            thr_r = thr_b[rs]

            def body(c, acc, rs=rs, thr_r=thr_r):
                for t in range(n_tile):
                    x = scr[c, rs, tile(t)]
                    acc = acc + jnp.where(x > thr_r if strict else x >= thr_r, one, zero)
                return acc

            acc = lax.fori_loop(0, i + 1, body, jnp.zeros((group, LANES), one.dtype))
            parts.append(jnp.sum(acc.astype(F32), axis=1, keepdims=True))
        return jnp.concatenate(parts, axis=0)

    one_h, zero_h = jnp.ones((), hi_scr.dtype), jnp.zeros((), hi_scr.dtype)
    one_f, zero_f = jnp.ones((), F32), jnp.zeros((), F32)
    kf = float(k_sel)
    u, cnt = _bisect(lambda t: count(hi_scr, wide(t).astype(hi_scr.dtype), one_h, zero_h), *_bisect_start(qb), 31, 16, kf)
    u, cnt_ge = _bisect(lambda t: count(s_scr, wide(t), one_f, zero_f), u, cnt, 15, 16, kf)
    qpos = i * qb + lax.broadcasted_iota(jnp.int32, (qb, 1), 0)
    short = qpos < k_sel
    thr = wide(jnp.where(short, -jnp.inf, _ordered_key_to_float(u)))
    need = wide(kf - count(s_scr, thr, one_f, zero_f, strict=True))
    has_tie = jnp.logical_and(jnp.logical_not(short), cnt_ge > kf)
    any_tie = jnp.max(jnp.where(has_tie, 1.0, 0.0)) > 0.5

    def attn_chunk(c, tie_seen, diag, ties):
        if ties:
            eq_sum = jnp.zeros((qb, LANES), F32)
            for t in range(n_tile):
                eq_f = jnp.where(s_scr[c, :, tile(t)] == thr, one_f, zero_f)
                p_scr[0, :, tile(t)] = eq_f.astype(p_scr.dtype)
                eq_sum = eq_sum + eq_f
            lg_scr[0] = _dot(p_scr[0], tri_ref[...])
        for t in range(n_tile):
            s = s_scr[c, :, tile(t)]
            if ties:
                first = tie_seen + lg_scr[0, :, tile(t)] < need
                sel = jnp.logical_or(s > thr, jnp.logical_and(s == thr, first))
            else:
                sel = s >= thr
            if diag:
                sel = jnp.logical_and(sel, causal[t])
            bias_scr[:, tile(t)] = jnp.where(sel, 0.0, NEG_INF)
        if ties:
            tie_seen = tie_seen + wide(jnp.sum(eq_sum, axis=1, keepdims=True))
        kc = key_rows(kb_ref, c)
        vc = key_rows(vb_ref, c)
        one_v = jnp.ones((), vc.dtype)
        v_aug = (jnp.where(lo_half, vc, one_v), jnp.where(lo_half, one_v, vc))
        for hh in range(N_HEADS):
            lg_scr[hh] = _dot_nt(qpad_scr[hh], kc)
        for hh in range(N_HEADS):
            for rb in range(qb // ROW_BLOCK):
                rows = slice(ROW_BLOCK * rb, ROW_BLOCK * (rb + 1))
                x = [lg_scr[hh, rows, tile(t)] + bias_scr[rows, tile(t)] for t in range(n_tile)]
                m_old = m_scr[hh, rows, :]
                m_new = jnp.maximum(m_old, jnp.max(functools.reduce(jnp.maximum, x), axis=1, keepdims=True))
                for t in range(n_tile):
                    p_scr[hh, rows, tile(t)] = jnp.exp2(x[t] - m_new).astype(p_scr.dtype)
                alpha_scr[hh, rows, :] = jnp.exp2(m_old - m_new)
                m_scr[hh, rows, :] = m_new
        for hh in range(N_HEADS):
            acc_scr[hh] = acc_scr[hh] * alpha_scr[hh] + _dot(p_scr[hh], v_aug[hh // KV_GROUP])
        return tie_seen

    def run(ties):
        acc_scr[...] = jnp.zeros_like(acc_scr)
        m_scr[...] = jnp.full_like(m_scr, NEG_INF)
        seen = lax.fori_loop(0, i, lambda c, ts: attn_chunk(c, ts, False, ties), jnp.zeros((qb, LANES), F32))
        attn_chunk(i, seen, True, ties)
        for j in range(KV_GROUP):
            a0 = acc_scr[j]
            a1 = acc_scr[KV_GROUP + j]
            o0 = a0 / pltpu.roll(a0, HEAD_DIM, 1)
            o1 = a1 / pltpu.roll(a1, HEAD_DIM, 1)
            a_ref[:, tile(j)] = jnp.where(lo_half, o0, o1).astype(a_ref.dtype)

    @pl.when(any_tie)
    def _():
        run(True)

    @pl.when(jnp.logical_not(any_tie))
    def _():
        run(False)


def _prompt_attn(q, iq, iw, kb, vb, ikb, tri, batch, seq, qb, k_sel):
    n = q.shape[0]
    nq = seq // qb
    qrow = lambda b, i: (b * nq + i, 0)
    keys = lambda b, i: (b, 0)
    return pl.pallas_call(
        functools.partial(_prompt_attn_kernel, k_sel=k_sel),
        grid=(batch, nq),
        in_specs=[
            pl.BlockSpec((qb, 512), qrow),
            pl.BlockSpec((qb, 256), qrow),
            pl.BlockSpec((qb, LANES), qrow),
            pl.BlockSpec((seq, LANES), keys),
            pl.BlockSpec((seq, LANES), keys),
            pl.BlockSpec((seq, LANES), keys),
            pl.BlockSpec((qb, qb), lambda b, i: (0, 0)),
        ],
        out_specs=pl.BlockSpec((qb, 512), qrow),
        out_shape=jax.ShapeDtypeStruct((n, 512), MXU_DTYPE),
        scratch_shapes=[
            pltpu.VMEM((nq, qb, qb), F32),
            pltpu.VMEM((nq, qb, qb), jnp.bfloat16),
            pltpu.VMEM((N_HEADS, qb, LANES), MXU_DTYPE),
            pltpu.VMEM((N_IDX_HEADS, qb, LANES), MXU_DTYPE),
            pltpu.VMEM((N_IDX_HEADS, qb, LANES), F32),
            pltpu.VMEM((qb, qb), F32),
            pltpu.VMEM((N_HEADS, qb, qb), F32),
            pltpu.VMEM((N_HEADS, qb, qb), MXU_DTYPE),
            pltpu.VMEM((N_HEADS, qb, LANES), F32),
            pltpu.VMEM((N_HEADS, qb, LANES), F32),
            pltpu.VMEM((N_HEADS, qb, LANES), F32),
        ],
        compiler_params=_params(2),
        name="prompt_attn",
    )(q, iq, iw, kb, vb, ikb, tri)


def _sample_attn_kernel(pt_ref, q_ref, iq_ref, iw_ref, k_ref, v_ref, ik_ref, tri_ref, ck_ref, cv_ref, cik_ref,
                        a_ref, kt_buf, vt_buf, ikt_buf, sems, *, layer, rps, t_new, n_pages, page, k_sel):
    step = pl.program_id(0)
    n_steps = pl.num_programs(0)
    slot = step % 2
    past = n_pages * page
    n_tile = past // LANES

    def page_copies(st, sl):
        copies = []
        for r in range(rps):
            for p in range(n_pages):
                phys = pt_ref[st * rps + r, p]
                dst = pl.ds(p * page, page)
                copies.append(pltpu.make_async_copy(ck_ref.at[layer, phys], kt_buf.at[sl, r, :, :, dst], sems.at[sl, 0]))
                copies.append(pltpu.make_async_copy(cv_ref.at[layer, phys], vt_buf.at[sl, r, :, :, dst], sems.at[sl, 1]))
                copies.append(pltpu.make_async_copy(cik_ref.at[layer, phys], ikt_buf.at[sl, r, :, dst], sems.at[sl, 2]))
        return copies

    @pl.when(step == 0)
    def _():
        for cp in page_copies(0, 0):
            cp.start()

    @pl.when(step + 1 < n_steps)
    def _():
        for cp in page_copies(step + 1, 1 - slot):
            cp.start()

    for cp in page_copies(step, slot):
        cp.wait()

    trow = lax.broadcasted_iota(jnp.int32, (t_new, LANES), 0)
    tcol = lax.broadcasted_iota(jnp.int32, (t_new, LANES), 1)
    causal_new = tcol <= trow
    tri = tri_ref[...]
    qpos = past + lax.broadcasted_iota(jnp.int32, (t_new, 1), 0)
    short = qpos < k_sel

    def pad_keys(x):
        return jnp.concatenate([x, jnp.zeros((LANES - t_new, x.shape[1]), x.dtype)], axis=0).astype(MXU_DTYPE)

    for r in range(rps):
        rows = slice(r * t_new, (r + 1) * t_new)
        iq = iq_ref[rows, :]
        iq4 = jnp.concatenate([iq[:, IDX_DIM * h:IDX_DIM * (h + 1)] for h in range(N_IDX_HEADS)], axis=0)
        iq4 = iq4.astype(MXU_DTYPE)
        iw = iw_ref[rows, :]
        wcol = [iw[:, h:h + 1] for h in range(N_IDX_HEADS)]
        zval = _zero_value(wcol)
        d_past = _dot(iq4, ikt_buf[slot, r].astype(MXU_DTYPE))
        d_new = _dot_nt(iq4, pad_keys(ik_ref[rows, :]))
        split = lambda d: [d[t_new * h:t_new * (h + 1)] for h in range(N_IDX_HEADS)]
        s_past = _indexer_scores(split(d_past), wcol, zval)
        s_new = jnp.where(causal_new, _indexer_scores(split(d_new), wcol, zval), NEG_INF)

        def count(pred):
            return (jnp.sum(jnp.where(pred(s_past), 1.0, 0.0), axis=1, keepdims=True)
                    + jnp.sum(jnp.where(pred(s_new), 1.0, 0.0), axis=1, keepdims=True))

        thr, _ = _threshold_search(lambda t: count(lambda s: s >= t), t_new, float(k_sel))
        thr = jnp.where(short, -jnp.inf, thr)
        need = float(k_sel) - count(lambda s: s > thr)

        eq_past = jnp.where(s_past == thr, 1.0, 0.0)
        eq_tiles = [eq_past[:, LANES * c:LANES * (c + 1)] for c in range(n_tile)]
        before = _dot(jnp.concatenate(eq_tiles, axis=0).astype(tri.dtype), tri)
        seen = jnp.zeros((t_new, 1), F32)
        sel_tiles = []
        for c in range(n_tile):
            sc = s_past[:, LANES * c:LANES * (c + 1)]
            rank = seen + before[t_new * c:t_new * (c + 1)]
            sel_tiles.append(jnp.where(jnp.logical_or(sc > thr, jnp.logical_and(sc == thr, rank < need)), 0.0, NEG_INF))
            seen = seen + jnp.sum(eq_tiles[c], axis=1, keepdims=True)
        bias_past = jnp.concatenate(sel_tiles, axis=1)
        eq_new = jnp.where(s_new == thr, 1.0, 0.0)
        rank_new = seen + _dot(eq_new.astype(tri.dtype), tri)
        sel_new = jnp.logical_or(s_new > thr, jnp.logical_and(s_new == thr, rank_new < need))
        sel_new = jnp.logical_and(sel_new, causal_new)

        q = q_ref[rows, :]
        for g in range(N_KV_HEADS):
            lo = HEAD_DIM * g
            qg = jnp.concatenate([q[:, LANES * j + lo:LANES * j + lo + HEAD_DIM] for j in range(KV_GROUP)], axis=0)
            qg = qg.astype(MXU_DTYPE)
            lp = _dot(qg, kt_buf[slot, r, g].astype(MXU_DTYPE))
            ln = _dot_nt(qg, pad_keys(k_ref[rows, lo:lo + HEAD_DIM]))
            vt = vt_buf[slot, r, g].astype(MXU_DTYPE)
            vn = pad_keys(v_ref[rows, lo:lo + HEAD_DIM])
            for j in range(KV_GROUP):
                hs = slice(t_new * j, t_new * (j + 1))
                lpj = jnp.where(bias_past == 0.0, lp[hs], NEG_INF)
                lnj = jnp.where(sel_new, ln[hs], NEG_INF)
                mx = jnp.maximum(jnp.max(lpj, axis=1, keepdims=True), jnp.max(lnj, axis=1, keepdims=True))
                pp = jnp.exp2(lpj - mx)
                pn = jnp.exp2(lnj - mx)
                den = jnp.sum(pp, axis=1, keepdims=True) + jnp.sum(pn, axis=1, keepdims=True)
                o = _dot_nt(pp.astype(MXU_DTYPE), vt) + _dot(pn.astype(MXU_DTYPE), vn)
                a_ref[rows, LANES * j + lo:LANES * j + lo + HEAD_DIM] = (o / den).astype(a_ref.dtype)


def _sample_attn(page_table, q, iq, iw, k, v, ik, tri, ck_t, cv_t, cik_t, layer, rps, t_new, k_sel):
    n = q.shape[0]
    n_req, n_pages = page_table.shape
    page = ck_t.shape[-1]
    past = n_pages * page
    rows = rps * t_new
    blk = lambda w: pl.BlockSpec((rows, w), lambda s, pt: (s, 0))
    any_spec = pl.BlockSpec(memory_space=pl.ANY)
    return pl.pallas_call(
        functools.partial(_sample_attn_kernel, layer=layer, rps=rps, t_new=t_new, n_pages=n_pages, page=page,
                          k_sel=k_sel),
        grid_spec=pltpu.PrefetchScalarGridSpec(
            num_scalar_prefetch=1,
            grid=(n_req // rps,),
            in_specs=[blk(512), blk(256), blk(LANES), blk(LANES), blk(LANES), blk(IDX_DIM),
                      pl.BlockSpec((LANES, LANES), lambda s, pt: (0, 0)),
                      any_spec, any_spec, any_spec],
            out_specs=blk(512),
            scratch_shapes=[
                pltpu.VMEM((2, rps, N_KV_HEADS, HEAD_DIM, past), F32),
                pltpu.VMEM((2, rps, N_KV_HEADS, HEAD_DIM, past), F32),
                pltpu.VMEM((2, rps, IDX_DIM, past), F32),
                pltpu.SemaphoreType.DMA((2, 3)),
            ],
        ),
        out_shape=jax.ShapeDtypeStruct((n, 512), F32),
        compiler_params=_params(1),
        name="sample_attn",
    )(page_table, q, iq, iw, k, v, ik, tri, ck_t, cv_t, cik_t)


def _router_gates(x1, wr_ref, rb_ref):
    tm = x1.shape[0]
    logits = jnp.dot(x1, wr_ref[...], preferred_element_type=F32, precision=lax.Precision.HIGHEST)
    aff = jax.nn.sigmoid(logits)
    biased = aff + rb_ref[...]
    lane = lax.broadcasted_iota(jnp.int32, (tm, LANES), 1)
    member = lane % EXPERTS_PER_GROUP
    group = lane // EXPERTS_PER_GROUP

    def other_member(x, d):
        return jnp.where(member + d < EXPERTS_PER_GROUP, pltpu.roll(x, LANES - d, 1),
                         pltpu.roll(x, EXPERTS_PER_GROUP - d, 1))

    def other_group(x, d):
        sh = EXPERTS_PER_GROUP * d
        return jnp.where(group + d < N_GROUPS, pltpu.roll(x, LANES - sh, 1), pltpu.roll(x, N_EXPERTS - sh, 1))

    def beaten_count(x, other, pos, n):
        cnt = jnp.zeros(x.shape, F32)
        for d in range(1, n):
            o = other(x, d)
            wins = jnp.logical_or(o > x, jnp.logical_and(o == x, pos + d >= n))
            cnt = cnt + jnp.where(wins, 1.0, 0.0)
        return cnt

    def group_sum(x):
        tot = x
        for d in range(1, EXPERTS_PER_GROUP):
            tot = tot + other_member(x, d)
        return tot

    top2 = beaten_count(biased, other_member, member, EXPERTS_PER_GROUP) < float(TOP_K_EXPERTS)
    grp_score = group_sum(jnp.where(top2, biased, 0.0))
    best = beaten_count(grp_score, other_group, group, N_GROUPS) < 1.0
    sel = jnp.logical_and(jnp.logical_and(top2, best), lane < N_EXPERTS)
    den = group_sum(jnp.where(sel, aff, 0.0))
    return jnp.where(sel, aff / jnp.where(sel, den, 1.0), 0.0)


def _mix_kernel(*refs, alpha, sample, t_new, blocks_per_seq):
    if sample:
        (x_ref, a_ref, wm_ref, cw_ref, cbias_ref, wb0_ref, wb1_ref, wo_ref, g_ref, b_ref, wr_ref, rb_ref,
         sp1_ref, sp2_ref, x1_ref, gates_ref, u_ref, ubuf) = refs
    else:
        (x_ref, a_ref, wm_ref, cw_ref, cbias_ref, wb0_ref, wb1_ref, wo_ref, g_ref, b_ref, wr_ref, rb_ref,
         x1_ref, gates_ref, u_ref, ubuf) = refs
    tm = x_ref.shape[0]
    hist = SUBLANES
    x = x_ref[...]
    xb = x.astype(MXU_DTYPE)
    c0, c1, c2, c3, c4 = 0, CONV_DIM, 2 * CONV_DIM, 3 * CONV_DIM, 3 * CONV_DIM + x.shape[1]
    ch = _dot(xb, wm_ref[:, c0:c1])
    cc = _dot(xb, wm_ref[:, c2:c3])
    u = cc * ch
    if sample:
        ubuf[0:hist, :] = jnp.zeros((hist, CONV_DIM), F32)
    else:
        @pl.when(pl.program_id(0) % blocks_per_seq == 0)
        def _():
            ubuf[0:hist, :] = jnp.zeros((hist, CONV_DIM), F32)
    ubuf[hist:hist + tm, :] = u
    prev1 = ubuf[hist - 1:hist - 1 + tm, :]
    prev2 = ubuf[hist - 2:hist - 2 + tm, :]
    if sample:
        t = lax.broadcasted_iota(jnp.int32, (tm, 1), 0) % t_new
        prev1 = jnp.where(t >= 1, prev1, sp1_ref[...])
        prev2 = jnp.where(t >= 2, prev2, sp2_ref[...])
        u_ref[...] = u
    else:
        u_ref[0] = u[tm - hist:tm, :]
        ubuf[0:hist, :] = u[tm - hist:tm, :]
    cw = cw_ref[...]
    cy = cbias_ref[...] + prev2 * cw[0:1, :]
    cy = cy + prev1 * cw[1:2, :]
    cy = cy + u * cw[2:3, :]
    cb = _dot(xb, wm_ref[:, c1:c2])
    c = cb * cy
    a_d = _dot(a_ref[...].astype(MXU_DTYPE), wb0_ref[...])
    c_d = _dot(c.astype(MXU_DTYPE), wb1_ref[...])
    ga = _dot(xb, wm_ref[:, c3:c4])
    merged = jax.nn.sigmoid(ga) * a_d
    gc = _dot(xb, wm_ref[:, c4:])
    merged = merged + jax.nn.sigmoid(gc) * c_d
    mix = _dot(merged.astype(MXU_DTYPE), wo_ref[...])
    x1 = _layer_norm(alpha * x + mix, g_ref[...], b_ref[...])
    x1_ref[...] = x1
    gates_ref[...] = _router_gates(x1, wr_ref, rb_ref)


def _const_spec(shape, index):
    return pl.BlockSpec(shape, index, pipeline_mode=pl.Buffered(1))


def _mix(x, a, wm, conv_w, conv_b, wb0, wb1, wo, ln_g, ln_b, wr, rb, layer, tm, alpha, seq=None, sp1=None, sp2=None,
         t_new=None):
    n, d = x.shape
    sample = sp1 is not None
    row = lambda i: (i, 0)
    lay3 = lambda i: (layer, 0, 0)
    in_specs = [
        pl.BlockSpec((tm, d), row),
        pl.BlockSpec((tm, 512), row),
        _const_spec((None,) + wm.shape[1:], lay3),
        pl.BlockSpec((None, CONV_W, CONV_DIM), lay3),
        pl.BlockSpec((None, 1, CONV_DIM), lay3),
        _const_spec((None,) + wb0.shape[1:], lay3),
        _const_spec((None,) + wb1.shape[1:], lay3),
        _const_spec((None,) + wo.shape[1:], lay3),
        pl.BlockSpec((None, 1, d), lay3),
        pl.BlockSpec((None, 1, d), lay3),
        pl.BlockSpec(wr.shape, lambda i: (0, 0)),
        pl.BlockSpec(rb.shape, lambda i: (0, 0)),
    ]
    args = [x, a, wm, conv_w, conv_b, wb0, wb1, wo, ln_g, ln_b, wr, rb]
    if sample:
        in_specs += [pl.BlockSpec((None, tm, CONV_DIM), lambda i: (layer, i, 0))] * 2
        args += [sp1, sp2]
        u_shape = jax.ShapeDtypeStruct((n, CONV_DIM), F32)
        u_spec = pl.BlockSpec((tm, CONV_DIM), row)
        blocks_per_seq = 1
    else:
        blocks_per_seq = seq // tm
        u_shape = jax.ShapeDtypeStruct((n // seq, SUBLANES, CONV_DIM), F32)
        u_spec = pl.BlockSpec((1, SUBLANES, CONV_DIM), lambda i: (i // blocks_per_seq, 0, 0))
    return pl.pallas_call(
        functools.partial(_mix_kernel, alpha=alpha, sample=sample, t_new=t_new, blocks_per_seq=blocks_per_seq),
        grid=(n // tm,),
        in_specs=in_specs,
        out_specs=[pl.BlockSpec((tm, d), row), pl.BlockSpec((tm, LANES), row), u_spec],
        out_shape=[jax.ShapeDtypeStruct((n, d), F32), jax.ShapeDtypeStruct((n, LANES), F32), u_shape],
        scratch_shapes=[pltpu.VMEM((SUBLANES + tm, CONV_DIM), F32)],
        compiler_params=_params(1),
        name="mix_sample" if sample else "mix_prompt",
    )(*args)


def _moe_kernel(x_ref, gates_ref, wg_ref, wu_ref, wd_ref, g_ref, b_ref, o_ref, acc_ref, *, alpha):
    x1 = x_ref[...]
    xb = x1.astype(MXU_DTYPE)
    gates = gates_ref[...]
    for e in range(N_EXPERTS):
        h = jax.nn.silu(_dot(xb, wg_ref[e])) * _dot(xb, wu_ref[e])
        h = h * gates[:, e:e + 1]
        y = _dot(h.astype(MXU_DTYPE), wd_ref[e])
        if e == 0:
            acc_ref[...] = y
        else:
            acc_ref[...] += y
    o_ref[...] = _layer_norm(alpha * x1 + acc_ref[...], g_ref[...], b_ref[...])


def _moe(x1, gates, wg, wu, wd, ln_g, ln_b, layer, tm, alpha):
    n, d = x1.shape
    row = lambda i: (i, 0)
    lay4 = lambda i: (layer, 0, 0, 0)
    lay3 = lambda i: (layer, 0, 0)
    return pl.pallas_call(
        functools.partial(_moe_kernel, alpha=alpha),
        grid=(n // tm,),
        in_specs=[
            pl.BlockSpec((tm, d), row),
            pl.BlockSpec((tm, LANES), row),
            _const_spec((None,) + wg.shape[1:], lay4),
            _const_spec((None,) + wu.shape[1:], lay4),
            _const_spec((None,) + wd.shape[1:], lay4),
            pl.BlockSpec((None, 1, d), lay3),
            pl.BlockSpec((None, 1, d), lay3),
        ],
        out_specs=pl.BlockSpec((tm, d), row),
        out_shape=jax.ShapeDtypeStruct((n, d), F32),
        scratch_shapes=[pltpu.VMEM((tm, d), F32)],
        compiler_params=_params(1),
        name="moe",
    )(x1, gates, wg, wu, wd, ln_g, ln_b)


def _rope_tables(pos):
    half = HEAD_DIM // 2
    inv = 1.0 / (ROPE_THETA ** (jnp.arange(half, dtype=F32) / half))
    ang = pos.astype(F32)[:, None] * inv[None, :]
    cos = jnp.cos(ang)
    sin = jnp.sin(ang)
    reps = LANES // HEAD_DIM
    return jnp.tile(cos, (1, 2 * reps)), jnp.tile(jnp.concatenate([-sin, sin], axis=1), (1, reps))


def _block(n, target):
    b = min(n, target)
    assert n % b == 0, (n, target)
    return b


def kernel(x_prompt, x_sample, cache_k, cache_v, cache_ik, state_conv, page_table, w_in, conv_w, conv_b, w_branch, w_out, ln1_g, ln1_b, ln2_g, ln2_b, w_router, router_bias, w_exp_gate, w_exp_up, w_exp_down):
    depth, d_model, _ = w_in.shape
    batch, seq, _ = x_prompt.shape
    n_req, t_new, _ = x_sample.shape
    n_pages = page_table.shape[1]
    page = cache_k.shape[2]
    past = n_pages * page
    assert t_new >= CONV_W - 1 and t_new % SUBLANES == 0 and page % LANES == 0
    alpha = float((2 * depth) ** 0.25)
    k_prompt_sel = min(TOPK_MAX, seq // 4)
    k_sample_sel = min(TOPK_MAX, (past + t_new) // 4)

    o = np.concatenate([[0], np.cumsum(SPLIT_SIZES)])
    perm = np.asarray(HEAD_PERM)
    wq = w_in[:, :, o[0]:o[1]].reshape(depth, d_model, N_HEADS, HEAD_DIM)[:, :, perm].reshape(depth, d_model, -1)
    w_ik = w_in[:, :, o[4]:o[5]]
    w_iw = jnp.pad(w_in[:, :, o[5]:o[6]], ((0, 0), (0, 0), (0, LANES - N_IDX_HEADS)))
    wa = jnp.concatenate([wq, w_in[:, :, o[1]:o[4]], w_ik, w_ik, w_iw], axis=-1).astype(MXU_DTYPE)
    wm = w_in[:, :, o[6]:].astype(MXU_DTYPE)
    wb0 = w_branch[:, 0].reshape(depth, N_HEADS, HEAD_DIM, d_model)[:, perm].reshape(depth, -1, d_model)
    wb0 = wb0.astype(MXU_DTYPE)
    wb1 = w_branch[:, 1].astype(MXU_DTYPE)
    wo = w_out.astype(MXU_DTYPE)
    wr = jnp.pad(w_router.astype(F32), ((0, 0), (0, LANES - N_EXPERTS)))
    rb = jnp.pad(router_bias.astype(F32), (0, LANES - N_EXPERTS)).reshape(1, LANES)
    wg = w_exp_gate.astype(MXU_DTYPE)
    wu = w_exp_up.astype(MXU_DTYPE)
    wd = w_exp_down.astype(MXU_DTYPE)
    ln1g, ln1b, ln2g, ln2b = (a.reshape(depth, 1, d_model) for a in (ln1_g, ln1_b, ln2_g, ln2_b))
    cbias = conv_b.reshape(depth, 1, CONV_DIM)

    cos_p, sin_p = _rope_tables(jnp.arange(seq, dtype=jnp.int32))
    pos_s = jnp.tile(past + jnp.arange(t_new, dtype=jnp.int32), n_req)
    cos_s, sin_s = _rope_tables(pos_s)
    ck_t = jnp.transpose(cache_k, (0, 1, 3, 4, 2))
    cv_t = jnp.transpose(cache_v, (0, 1, 3, 4, 2))
    cik_t = jnp.transpose(cache_ik, (0, 1, 3, 2))
    zeros_state = jnp.zeros((depth, n_req, t_new, CONV_DIM), F32)
    sp1 = zeros_state.at[:, :, 0].set(state_conv[:, :, 1]).reshape(depth, n_req * t_new, CONV_DIM)
    sp2 = zeros_state.at[:, :, :2].set(state_conv).reshape(depth, n_req * t_new, CONV_DIM)

    qb = _block(seq, 256)
    tri = (np.arange(qb)[:, None] < np.arange(qb)[None, :])
    tri_p = jnp.asarray(tri, MXU_DTYPE)
    tri_s = jnp.asarray(tri[:LANES, :LANES], MXU_DTYPE)
    tm_p = _block(seq, 512)
    n_s = n_req * t_new
    tm_s = _block(n_s, 512)
    rps = 2 if n_req % 2 == 0 else 1

    xp = x_prompt.reshape(batch * seq, d_model)
    xs = x_sample.reshape(n_s, d_model)
    outs = [[] for _ in range(8)]
    for l in range(depth):
        q, k, v, iq, ik, iw, kb, vb, ikb = _attn_proj(xp, wa, l, cos_p, sin_p, tm_p, MXU_DTYPE)
        a = _prompt_attn(q, iq, iw, kb, vb, ikb, tri_p, batch, seq, qb, k_prompt_sel)
        x1, gates, u_last = _mix(xp, a, wm, conv_w, cbias, wb0, wb1, wo, ln1g, ln1b, wr, rb, l, tm_p, alpha, seq=seq)
        xp = _moe(x1, gates, wg, wu, wd, ln2g, ln2b, l, tm_p, alpha)
        outs[0].append(k.reshape(batch, seq, N_KV_HEADS, HEAD_DIM))
        outs[1].append(v.reshape(batch, seq, N_KV_HEADS, HEAD_DIM))
        outs[2].append(ik.reshape(batch, seq, IDX_DIM))
        outs[3].append(u_last[:, SUBLANES - (CONV_W - 1):])

        q, k, v, iq, ik, iw, _, _, _ = _attn_proj(xs, wa, l, cos_s, sin_s, tm_s, F32)
        a = _sample_attn(page_table, q, iq, iw, k, v, ik, tri_s, ck_t, cv_t, cik_t, l, rps, t_new, k_sample_sel)
        x1, gates, u = _mix(xs, a, wm, conv_w, cbias, wb0, wb1, wo, ln1g, ln1b, wr, rb, l, tm_s, alpha,
                            sp1=sp1, sp2=sp2, t_new=t_new)
        xs = _moe(x1, gates, wg, wu, wd, ln2g, ln2b, l, tm_s, alpha)
        outs[4].append(k.reshape(n_req, t_new, N_KV_HEADS, HEAD_DIM))
        outs[5].append(v.reshape(n_req, t_new, N_KV_HEADS, HEAD_DIM))
        outs[6].append(ik.reshape(n_req, t_new, IDX_DIM))
        outs[7].append(u.reshape(n_req, t_new, CONV_DIM)[:, t_new - (CONV_W - 1):])
    stacked = [jnp.stack(o_) for o_ in outs]
    return (xp.reshape(batch, seq, d_model), xs.reshape(n_req, t_new, d_model), *stacked)
```

```python
import functools

import numpy as np
import jax
import jax.numpy as jnp
from jax import lax
from jax.experimental import pallas as pl
from jax.experimental.pallas import tpu as pltpu

N_HEADS = 8
HEAD_DIM = 64
N_KV_HEADS = 2
KV_GROUP = N_HEADS // N_KV_HEADS
N_IDX_HEADS = 4
IDX_DIM = 64
TOPK_MAX = 256
ROPE_THETA = 10000.0
CONV_DIM = 512
CONV_W = 3
N_EXPERTS = 16
N_GROUPS = 4
EXPERTS_PER_GROUP = N_EXPERTS // N_GROUPS
TOP_K_EXPERTS = 2
D_EXPERT = 256
LN_EPS = 1e-5
NEG_INF = -1e30
SPLIT_SIZES = (N_HEADS * HEAD_DIM, N_KV_HEADS * HEAD_DIM, N_KV_HEADS * HEAD_DIM, N_IDX_HEADS * IDX_DIM,
               IDX_DIM, N_IDX_HEADS, CONV_DIM, CONV_DIM, CONV_DIM)

LANES = 128
SUBLANES = 8
VMEM_LIMIT = 56 * 1024 * 1024
F32 = jnp.float32
MXU_DTYPE = jnp.bfloat16
F32_MIN_NORMAL = float(np.finfo(np.float32).tiny)
LOG2E = float(np.log2(np.e))
ROW_BLOCK = 64
HEAD_PERM = tuple(g * KV_GROUP + j for j in range(KV_GROUP) for g in range(N_KV_HEADS))

_NT = (((1,), (1,)), ((), ()))


def _dot(a, b):
    return jnp.dot(a, b, preferred_element_type=F32)


def _dot_nt(a, b):
    return lax.dot_general(a, b, _NT, preferred_element_type=F32)


def _params(n_grid):
    return pltpu.CompilerParams(dimension_semantics=("arbitrary",) * n_grid, vmem_limit_bytes=VMEM_LIMIT)


def _layer_norm(y, g, b):
    mu = jnp.mean(y, axis=-1, keepdims=True)
    yc = y - mu
    var = jnp.mean(yc * yc, axis=-1, keepdims=True)
    return yc * lax.rsqrt(var + LN_EPS) * g + b


def _attn_proj_kernel(*refs, prompt):
    if prompt:
        (x_ref, w_ref, cos_ref, sin_ref, wiw_ref,
         q_ref, k_ref, v_ref, iq_ref, ik_ref, kb_ref, vt_ref, ikb_ref, iwt_ref) = refs
    else:
        x_ref, w_ref, cos_ref, sin_ref, q_ref, k_ref, v_ref, iq_ref, ik_ref, iw_ref = refs
    tm = x_ref.shape[0]
    xb = x_ref[...].astype(MXU_DTYPE)
    cos = cos_ref[...]
    sin = sin_ref[...]
    lane = lax.broadcasted_iota(jnp.int32, (tm, LANES), 1)
    first_half = (lane & (HEAD_DIM // 2)) == 0

    def proj(col):
        return _dot(xb, w_ref[:, col:col + LANES])

    def rope(g):
        partner = jnp.where(first_half, pltpu.roll(g, LANES - HEAD_DIM // 2, 1), pltpu.roll(g, HEAD_DIM // 2, 1))
        return g * cos + partner * sin

    for j in range(4):
        q_ref[:, LANES * j:LANES * (j + 1)] = (rope(proj(LANES * j)) * (HEAD_DIM ** -0.5 * LOG2E)).astype(q_ref.dtype)
    k = rope(proj(512))
    k_ref[...] = k
    v = proj(640)
    v_ref[...] = v
    for j in range(2):
        iq_ref[:, LANES * j:LANES * (j + 1)] = rope(proj(768 + LANES * j)).astype(iq_ref.dtype)
    ik2 = rope(proj(1024))
    ik_ref[...] = ik2[:, :IDX_DIM]
    iw_scale = N_IDX_HEADS ** -0.5 * IDX_DIM ** -0.5
    if prompt:
        kb_ref[...] = k.astype(kb_ref.dtype)
        ikb_ref[...] = ik2.astype(ikb_ref.dtype)
        chunk = vt_ref.shape[2]
        for j in range(tm // chunk):
            vt_ref[j] = v[chunk * j:chunk * (j + 1), :].T.astype(vt_ref.dtype)
        iwt_ref[...] = _dot_nt(wiw_ref[...], xb) * iw_scale
    else:
        iw_ref[...] = proj(1152) * iw_scale


def _attn_proj(x, wa, layer, cos, sin, tm, wiw_t=None, chunk=None):
    n, d = x.shape
    prompt = wiw_t is not None
    nblk_pos = cos.shape[0] // tm
    row = lambda i: (i, 0)
    pos = lambda i: (i % nblk_pos, 0)
    low = MXU_DTYPE if prompt else F32
    outs = [((n, 512), low, (tm, 512), row),
            ((n, LANES), F32, (tm, LANES), row),
            ((n, LANES), F32, (tm, LANES), row),
            ((n, 256), low, (tm, 256), row),
            ((n, IDX_DIM), F32, (tm, IDX_DIM), row)]
    in_specs = [
        pl.BlockSpec((tm, d), row),
        pl.BlockSpec((None, d, wa.shape[2]), lambda i: (layer, 0, 0)),
        pl.BlockSpec((tm, LANES), pos),
        pl.BlockSpec((tm, LANES), pos),
    ]
    args = [x, wa, cos, sin]
    if prompt:
        in_specs.append(pl.BlockSpec((None,) + wiw_t.shape[1:], lambda i: (layer, 0, 0)))
        args.append(wiw_t)
        outs += [((n, LANES), MXU_DTYPE, (tm, LANES), row),
                 ((n // chunk, LANES, chunk), MXU_DTYPE, (tm // chunk, LANES, chunk), lambda i: (i, 0, 0)),
                 ((n, LANES), MXU_DTYPE, (tm, LANES), row),
                 ((SUBLANES, n), F32, (SUBLANES, tm), lambda i: (0, i))]
    else:
        outs.append(((n, LANES), F32, (tm, LANES), row))
    return pl.pallas_call(
        functools.partial(_attn_proj_kernel, prompt=prompt),
        grid=(n // tm,),
        in_specs=in_specs,
        out_specs=[pl.BlockSpec(blk, idx) for _, _, blk, idx in outs],
        out_shape=[jax.ShapeDtypeStruct(shape, dt) for shape, dt, _, _ in outs],
        compiler_params=_params(1),
        name="attn_proj_prompt" if prompt else "attn_proj_sample",
    )(*args)


def _ordered_key_to_float(u):
    int_min = jnp.int32(-2 ** 31)
    ck = u ^ int_min
    bits = jnp.where(ck >= 0, ck, ck ^ jnp.int32(2 ** 31 - 1))
    return lax.bitcast_convert_type(bits, F32)


def _bisect(count_ge, u, cnt_u, first_bit, n_bits, k_sel):
    def bit_body(it, carry):
        u, cnt_u = carry
        bit = lax.shift_left(jnp.int32(1), jnp.int32(first_bit) - it)
        cu = u | bit
        cnt = count_ge(_ordered_key_to_float(cu))
        take = cnt >= k_sel
        return jnp.where(take, cu, u), jnp.where(take, cnt, cnt_u)

    return lax.fori_loop(0, n_bits, bit_body, (u, cnt_u))


def _bisect_start(rows):
    return jnp.zeros((rows, 1), jnp.int32), jnp.full((rows, 1), 3e38, F32)


def _threshold_search(count_ge, rows, k_sel):
    u, cnt = _bisect(count_ge, *_bisect_start(rows), 31, 32, k_sel)
    return _ordered_key_to_float(u), cnt


def _floor_to_bf16(x):
    b = lax.bitcast_convert_type(x, jnp.int32)
    away = lax.shift_right_arithmetic(b, jnp.int32(31)) & jnp.int32(0xFFFF)
    return lax.bitcast_convert_type((b + away) & jnp.int32(-65536), F32).astype(jnp.bfloat16)


def _indexer_scores(dots, wcol, zval):
    s = None
    for d, w in zip(dots, wcol):
        t = jnp.maximum(d, 0.0) * w
        s = t if s is None else s + t
    return jnp.where(s == 0.0, zval, s)


def _zero_value(wcol):
    wmax = functools.reduce(jnp.maximum, wcol)
    return jnp.where(wmax < 0.0, -F32_MIN_NORMAL, 0.0).astype(F32)


def _prompt_attn_kernel(q_ref, iq_ref, iwt_ref, kb_ref, vt_ref, ikb_ref, tril_ref, a_ref,
                        s_scr, hi_scr, qpad_scr, iqp_scr, bias_scr, tie_scr, lg_scr, p_scr, m_scr, acc_scr,
                        *, k_sel):
    qb = q_ref.shape[0]
    i = pl.program_id(1)
    kf = float(k_sel)
    lane = lax.broadcasted_iota(jnp.int32, (qb, LANES), 1)
    lo_half = lane < HEAD_DIM
    halves = (lo_half, jnp.logical_not(lo_half))
    tile = lambda t: slice(LANES * t, LANES * (t + 1))
    blocks = [slice(r, r + ROW_BLOCK) for r in range(0, qb, ROW_BLOCK)]

    def causal(rows):
        key = rows.start + lax.broadcasted_iota(jnp.int32, (ROW_BLOCK, qb), 0)
        return key <= lax.broadcasted_iota(jnp.int32, (ROW_BLOCK, qb), 1)

    for h in range(N_IDX_HEADS):
        iqp_scr[h] = jnp.where(halves[h % 2], iq_ref[:, tile(h // 2)], jnp.zeros((), iq_ref.dtype))
    for hh in range(N_HEADS):
        g, j = divmod(hh, KV_GROUP)
        qpad_scr[hh] = jnp.where(halves[g], q_ref[:, tile(j)], jnp.zeros((), q_ref.dtype))
    iwt = iwt_ref[...]
    wrow = [iwt[h:h + 1, :] for h in range(N_IDX_HEADS)]
    zval = _zero_value(wrow)

    def key_rows(ref, c):
        return ref[pl.ds(pl.multiple_of(c * qb, qb), qb), :]

    def score_chunk(c, diag):
        ikc = key_rows(ikb_ref, c)
        for h in range(N_IDX_HEADS):
            lg_scr[h] = _dot_nt(ikc, iqp_scr[h])
        for rows in blocks:
            s = None
            for h in range(N_IDX_HEADS):
                term = jnp.maximum(lg_scr[h, rows, :], 0.0) * wrow[h]
                s = term if s is None else s + term
            s = jnp.where(s == 0.0, zval, s)
            if diag:
                s = jnp.where(causal(rows), s, NEG_INF)
            s_scr[c, rows, :] = s
            hi_scr[c, rows, :] = _floor_to_bf16(s)

    def score_body(c, carry):
        score_chunk(c, False)
        return carry

    lax.fori_loop(0, i, score_body, 0)
    score_chunk(i, True)

    def count(scr, thr_row, one, zero, strict=False):
        thr_b = jnp.broadcast_to(thr_row, (ROW_BLOCK, qb))

        def body(c, acc):
            for rows in blocks:
                x = scr[c, rows, :]
                acc = acc + jnp.where(x > thr_b if strict else x >= thr_b, one, zero)
            return acc

        acc = lax.fori_loop(0, i + 1, body, jnp.zeros((ROW_BLOCK, qb), one.dtype))
        return jnp.sum(acc.astype(F32), axis=0, keepdims=True)

    one_h, zero_h = jnp.ones((), hi_scr.dtype), jnp.zeros((), hi_scr.dtype)
    one_f, zero_f = jnp.ones((), F32), jnp.zeros((), F32)
    qpos = i * qb + lax.broadcasted_iota(jnp.int32, (1, qb), 1)
    short = qpos < k_sel

    def unresolved(cnt):
        return jnp.max(jnp.where(jnp.logical_or(short, cnt == kf), 0, 1)).astype(jnp.int32)

    u, cnt = _bisect(lambda t: count(hi_scr, t.astype(hi_scr.dtype), one_h, zero_h),
                     jnp.zeros((1, qb), jnp.int32), jnp.full((1, qb), 3e38, F32), 31, 16, kf)

    def low_bits_body(carry):
        it, u, cnt, _ = carry
        for _ in range(2):
            cu = u | lax.shift_left(jnp.int32(1), jnp.int32(15) - it)
            c = count(s_scr, _ordered_key_to_float(cu), one_f, zero_f)
            take = c >= kf
            u, cnt, it = jnp.where(take, cu, u), jnp.where(take, c, cnt), it + 1
        return it, u, cnt, unresolved(cnt)

    _, u, cnt_ge, _ = lax.while_loop(lambda cr: jnp.logical_and(cr[0] < 16, cr[3] > 0), low_bits_body,
                                     (jnp.int32(0), u, cnt, unresolved(cnt)))
    thr = jnp.where(short, -jnp.inf, _ordered_key_to_float(u))
    need = kf - count(s_scr, thr, one_f, zero_f, strict=True)
    has_tie = jnp.logical_and(jnp.logical_not(short), cnt_ge > kf)
    any_tie = jnp.max(jnp.where(has_tie, 1.0, 0.0)) > 0.5

    def attn_chunk(c, tie_seen, diag, ties):
        if ties:
            eq = jnp.where(s_scr[c] == thr, one_f, zero_f)
            p_scr[0] = eq.astype(p_scr.dtype)
            tie_scr[...] = _dot(tril_ref[...], p_scr[0])
        for rows in blocks:
            s = s_scr[c, rows, :]
            if ties:
                first = tie_seen + tie_scr[rows, :] < need
                sel = jnp.logical_or(s > thr, jnp.logical_and(s == thr, first))
            else:
                sel = s >= thr
            if diag:
                sel = jnp.logical_and(sel, causal(rows))
            bias_scr[rows, :] = jnp.where(sel, 0.0, NEG_INF)
        if ties:
            tie_seen = tie_seen + jnp.sum(eq, axis=0, keepdims=True)
        vt = vt_ref[c]
        top = lax.broadcasted_iota(jnp.int32, vt.shape, 0) < HEAD_DIM
        one_v = jnp.ones((), vt.dtype)
        vt_aug = (jnp.where(top, vt, one_v), jnp.where(top, one_v, vt))
        kc = key_rows(kb_ref, c)
        for hh in range(N_HEADS):
            lg_scr[hh] = _dot_nt(kc, qpad_scr[hh])
        alpha = []
        for hh in range(N_HEADS):
            top_x = None
            for rows in blocks:
                x = lg_scr[hh, rows, :] + bias_scr[rows, :]
                top_x = x if top_x is None else jnp.maximum(top_x, x)
            m_old = m_scr[hh]
            m_new = jnp.maximum(m_old, jnp.max(top_x, axis=0, keepdims=True))
            for rows in blocks:
                x = lg_scr[hh, rows, :] + bias_scr[rows, :]
                p_scr[hh, rows, :] = jnp.exp2(x - m_new).astype(p_scr.dtype)
            alpha.append(jnp.exp2(m_old - m_new))
            m_scr[hh] = m_new
        for hh in range(N_HEADS):
            acc_scr[hh] = acc_scr[hh] * alpha[hh] + _dot(vt_aug[hh // KV_GROUP], p_scr[hh])
        return tie_seen

    def run(ties):
        acc_scr[...] = jnp.zeros_like(acc_scr)
        m_scr[...] = jnp.full_like(m_scr, NEG_INF)
        seen = lax.fori_loop(0, i, lambda c, ts: attn_chunk(c, ts, False, ties), jnp.zeros((1, qb), F32))
        attn_chunk(i, seen, True, ties)
        for j in range(KV_GROUP):
            a0 = acc_scr[j]
            a1 = acc_scr[KV_GROUP + j]
            o0 = a0[:HEAD_DIM] / a0[HEAD_DIM:HEAD_DIM + 1]
            o1 = a1[HEAD_DIM:] / a1[0:1]
            a_ref[:, tile(j)] = jnp.concatenate([o0, o1], axis=0).T.astype(a_ref.dtype)

    @pl.when(any_tie)
    def _():
        run(True)

    @pl.when(jnp.logical_not(any_tie))
    def _():
        run(False)


def _prompt_attn(q, iq, iwt, kb, vt, ikb, tril, batch, seq, qb, k_sel):
    n = q.shape[0]
    nq = seq // qb
    assert nq * (qb // ROW_BLOCK) <= 256 and qb % ROW_BLOCK == 0
    qrow = lambda b, i: (b * nq + i, 0)
    keys = lambda b, i: (b, 0)
    return pl.pallas_call(
        functools.partial(_prompt_attn_kernel, k_sel=k_sel),
        grid=(batch, nq),
        in_specs=[
            pl.BlockSpec((qb, 512), qrow),
            pl.BlockSpec((qb, 256), qrow),
            pl.BlockSpec((SUBLANES, qb), lambda b, i: (0, b * nq + i)),
            pl.BlockSpec((seq, LANES), keys),
            pl.BlockSpec((nq, LANES, qb), lambda b, i: (b, 0, 0)),
            pl.BlockSpec((seq, LANES), keys),
            pl.BlockSpec((qb, qb), lambda b, i: (0, 0)),
        ],
        out_specs=pl.BlockSpec((qb, 512), qrow),
        out_shape=jax.ShapeDtypeStruct((n, 512), MXU_DTYPE),
        scratch_shapes=[
            pltpu.VMEM((nq, qb, qb), F32),
            pltpu.VMEM((nq, qb, qb), jnp.bfloat16),
            pltpu.VMEM((N_HEADS, qb, LANES), MXU_DTYPE),
            pltpu.VMEM((N_IDX_HEADS, qb, LANES), MXU_DTYPE),
            pltpu.VMEM((qb, qb), F32),
            pltpu.VMEM((qb, qb), F32),
            pltpu.VMEM((N_HEADS, qb, qb), F32),
            pltpu.VMEM((N_HEADS, qb, qb), MXU_DTYPE),
            pltpu.VMEM((N_HEADS, 1, qb), F32),
            pltpu.VMEM((N_HEADS, LANES, qb), F32),
        ],
        compiler_params=_params(2),
        name="prompt_attn",
    )(q, iq, iwt, kb, vt, ikb, tril)


def _sample_attn_kernel(pt_ref, q_ref, iq_ref, iw_ref, k_ref, v_ref, ik_ref, tri_ref, ck_ref, cv_ref, cik_ref,
                        a_ref, kt_buf, vt_buf, ikt_buf, sems, *, layer, rps, t_new, n_pages, page, k_sel):
    step = pl.program_id(0)
    n_steps = pl.num_programs(0)
    slot = step % 2
    past = n_pages * page
    n_tile = past // LANES

    def page_copies(st, sl):
        copies = []
        for r in range(rps):
            for p in range(n_pages):
                phys = pt_ref[st * rps + r, p]
                dst = pl.ds(p * page, page)
                copies.append(pltpu.make_async_copy(ck_ref.at[layer, phys], kt_buf.at[sl, r, :, :, dst], sems.at[sl, 0]))
                copies.append(pltpu.make_async_copy(cv_ref.at[layer, phys], vt_buf.at[sl, r, :, :, dst], sems.at[sl, 1]))
                copies.append(pltpu.make_async_copy(cik_ref.at[layer, phys], ikt_buf.at[sl, r, :, dst], sems.at[sl, 2]))
        return copies

    @pl.when(step == 0)
    def _():
        for cp in page_copies(0, 0):
            cp.start()

    @pl.when(step + 1 < n_steps)
    def _():
        for cp in page_copies(step + 1, 1 - slot):
            cp.start()

    for cp in page_copies(step, slot):
        cp.wait()

    trow = lax.broadcasted_iota(jnp.int32, (t_new, LANES), 0)
    tcol = lax.broadcasted_iota(jnp.int32, (t_new, LANES), 1)
    causal_new = tcol <= trow
    tri = tri_ref[...]
    qpos = past + lax.broadcasted_iota(jnp.int32, (t_new, 1), 0)
    short = qpos < k_sel

    def pad_keys(x):
        return jnp.concatenate([x, jnp.zeros((LANES - t_new, x.shape[1]), x.dtype)], axis=0).astype(MXU_DTYPE)

    for r in range(rps):
        rows = slice(r * t_new, (r + 1) * t_new)
        iq = iq_ref[rows, :]
        iq4 = jnp.concatenate([iq[:, IDX_DIM * h:IDX_DIM * (h + 1)] for h in range(N_IDX_HEADS)], axis=0)
        iq4 = iq4.astype(MXU_DTYPE)
        iw = iw_ref[rows, :]
        wcol = [iw[:, h:h + 1] for h in range(N_IDX_HEADS)]
        zval = _zero_value(wcol)
        d_past = _dot(iq4, ikt_buf[slot, r].astype(MXU_DTYPE))
        d_new = _dot_nt(iq4, pad_keys(ik_ref[rows, :]))
        split = lambda d: [d[t_new * h:t_new * (h + 1)] for h in range(N_IDX_HEADS)]
        s_past = _indexer_scores(split(d_past), wcol, zval)
        s_new = jnp.where(causal_new, _indexer_scores(split(d_new), wcol, zval), NEG_INF)

        def count(pred):
            return (jnp.sum(jnp.where(pred(s_past), 1.0, 0.0), axis=1, keepdims=True)
                    + jnp.sum(jnp.where(pred(s_new), 1.0, 0.0), axis=1, keepdims=True))

        thr, _ = _threshold_search(lambda t: count(lambda s: s >= t), t_new, float(k_sel))
        thr = jnp.where(short, -jnp.inf, thr)
        need = float(k_sel) - count(lambda s: s > thr)

        eq_past = jnp.where(s_past == thr, 1.0, 0.0)
        eq_tiles = [eq_past[:, LANES * c:LANES * (c + 1)] for c in range(n_tile)]
        before = _dot(jnp.concatenate(eq_tiles, axis=0).astype(tri.dtype), tri)
        seen = jnp.zeros((t_new, 1), F32)
        sel_tiles = []
        for c in range(n_tile):
            sc = s_past[:, LANES * c:LANES * (c + 1)]
            rank = seen + before[t_new * c:t_new * (c + 1)]
            sel_tiles.append(jnp.where(jnp.logical_or(sc > thr, jnp.logical_and(sc == thr, rank < need)), 0.0, NEG_INF))
            seen = seen + jnp.sum(eq_tiles[c], axis=1, keepdims=True)
        bias_past = jnp.concatenate(sel_tiles, axis=1)
        eq_new = jnp.where(s_new == thr, 1.0, 0.0)
        rank_new = seen + _dot(eq_new.astype(tri.dtype), tri)
        sel_new = jnp.logical_or(s_new > thr, jnp.logical_and(s_new == thr, rank_new < need))
        sel_new = jnp.logical_and(sel_new, causal_new)

        q = q_ref[rows, :]
        for g in range(N_KV_HEADS):
            lo = HEAD_DIM * g
            qg = jnp.concatenate([q[:, LANES * j + lo:LANES * j + lo + HEAD_DIM] for j in range(KV_GROUP)], axis=0)
            qg = qg.astype(MXU_DTYPE)
            lp = _dot(qg, kt_buf[slot, r, g].astype(MXU_DTYPE))
            ln = _dot_nt(qg, pad_keys(k_ref[rows, lo:lo + HEAD_DIM]))
            vt = vt_buf[slot, r, g].astype(MXU_DTYPE)
            vn = pad_keys(v_ref[rows, lo:lo + HEAD_DIM])
            for j in range(KV_GROUP):
                hs = slice(t_new * j, t_new * (j + 1))
                lpj = jnp.where(bias_past == 0.0, lp[hs], NEG_INF)
                lnj = jnp.where(sel_new, ln[hs], NEG_INF)
                mx = jnp.maximum(jnp.max(lpj, axis=1, keepdims=True), jnp.max(lnj, axis=1, keepdims=True))
                pp = jnp.exp2(lpj - mx)
                pn = jnp.exp2(lnj - mx)
                den = jnp.sum(pp, axis=1, keepdims=True) + jnp.sum(pn, axis=1, keepdims=True)
                o = _dot_nt(pp.astype(MXU_DTYPE), vt) + _dot(pn.astype(MXU_DTYPE), vn)
                a_ref[rows, LANES * j + lo:LANES * j + lo + HEAD_DIM] = (o / den).astype(a_ref.dtype)


def _sample_attn(page_table, q, iq, iw, k, v, ik, tri, ck_t, cv_t, cik_t, layer, rps, t_new, k_sel):
    n = q.shape[0]
    n_req, n_pages = page_table.shape
    page = ck_t.shape[-1]
    past = n_pages * page
    rows = rps * t_new
    blk = lambda w: pl.BlockSpec((rows, w), lambda s, pt: (s, 0))
    any_spec = pl.BlockSpec(memory_space=pl.ANY)
    return pl.pallas_call(
        functools.partial(_sample_attn_kernel, layer=layer, rps=rps, t_new=t_new, n_pages=n_pages, page=page,
                          k_sel=k_sel),
        grid_spec=pltpu.PrefetchScalarGridSpec(
            num_scalar_prefetch=1,
            grid=(n_req // rps,),
            in_specs=[blk(512), blk(256), blk(LANES), blk(LANES), blk(LANES), blk(IDX_DIM),
                      pl.BlockSpec((LANES, LANES), lambda s, pt: (0, 0)),
                      any_spec, any_spec, any_spec],
            out_specs=blk(512),
            scratch_shapes=[
                pltpu.VMEM((2, rps, N_KV_HEADS, HEAD_DIM, past), F32),
                pltpu.VMEM((2, rps, N_KV_HEADS, HEAD_DIM, past), F32),
                pltpu.VMEM((2, rps, IDX_DIM, past), F32),
                pltpu.SemaphoreType.DMA((2, 3)),
            ],
        ),
        out_shape=jax.ShapeDtypeStruct((n, 512), F32),
        compiler_params=_params(1),
        name="sample_attn",
    )(page_table, q, iq, iw, k, v, ik, tri, ck_t, cv_t, cik_t)


def _router_gates(x1, wr_ref, rb_ref):
    tm = x1.shape[0]
    logits = jnp.dot(x1, wr_ref[...], preferred_element_type=F32, precision=lax.Precision.HIGHEST)
    aff = jax.nn.sigmoid(logits)
    biased = aff + rb_ref[...]
    lane = lax.broadcasted_iota(jnp.int32, (tm, LANES), 1)
    member = lane % EXPERTS_PER_GROUP
    group = lane // EXPERTS_PER_GROUP

    def other_member(x, d):
        return jnp.where(member + d < EXPERTS_PER_GROUP, pltpu.roll(x, LANES - d, 1),
                         pltpu.roll(x, EXPERTS_PER_GROUP - d, 1))

    def other_group(x, d):
        sh = EXPERTS_PER_GROUP * d
        return jnp.where(group + d < N_GROUPS, pltpu.roll(x, LANES - sh, 1), pltpu.roll(x, N_EXPERTS - sh, 1))

    def beaten_count(x, other, pos, n):
        cnt = jnp.zeros(x.shape, F32)
        for d in range(1, n):
            o = other(x, d)
            wins = jnp.logical_or(o > x, jnp.logical_and(o == x, pos + d >= n))
            cnt = cnt + jnp.where(wins, 1.0, 0.0)
        return cnt

    def group_sum(x):
        tot = x
        for d in range(1, EXPERTS_PER_GROUP):
            tot = tot + other_member(x, d)
        return tot

    top2 = beaten_count(biased, other_member, member, EXPERTS_PER_GROUP) < float(TOP_K_EXPERTS)
    grp_score = group_sum(jnp.where(top2, biased, 0.0))
    best = beaten_count(grp_score, other_group, group, N_GROUPS) < 1.0
    sel = jnp.logical_and(jnp.logical_and(top2, best), lane < N_EXPERTS)
    den = group_sum(jnp.where(sel, aff, 0.0))
    return jnp.where(sel, aff / jnp.where(sel, den, 1.0), 0.0)


def _mix_kernel(*refs, alpha, sample, t_new, blocks_per_seq):
    if sample:
        (x_ref, a_ref, wm_ref, cw_ref, cbias_ref, wb0_ref, wb1_ref, wo_ref, g_ref, b_ref, wr_ref, rb_ref,
         sp1_ref, sp2_ref, x1_ref, gates_ref, u_ref, ubuf) = refs
    else:
        (x_ref, a_ref, wm_ref, cw_ref, cbias_ref, wb0_ref, wb1_ref, wo_ref, g_ref, b_ref, wr_ref, rb_ref,
         x1_ref, gates_ref, u_ref, ubuf) = refs
    tm = x_ref.shape[0]
    hist = SUBLANES
    x = x_ref[...]
    xb = x.astype(MXU_DTYPE)
    c0, c1, c2, c3, c4 = 0, CONV_DIM, 2 * CONV_DIM, 3 * CONV_DIM, 3 * CONV_DIM + x.shape[1]
    ch = _dot(xb, wm_ref[:, c0:c1])
    cc = _dot(xb, wm_ref[:, c2:c3])
    u = cc * ch
    if sample:
        ubuf[0:hist, :] = jnp.zeros((hist, CONV_DIM), F32)
    else:
        @pl.when(pl.program_id(0) % blocks_per_seq == 0)
        def _():
            ubuf[0:hist, :] = jnp.zeros((hist, CONV_DIM), F32)
    ubuf[hist:hist + tm, :] = u
    prev1 = ubuf[hist - 1:hist - 1 + tm, :]
    prev2 = ubuf[hist - 2:hist - 2 + tm, :]
    if sample:
        t = lax.broadcasted_iota(jnp.int32, (tm, 1), 0) % t_new
        prev1 = jnp.where(t >= 1, prev1, sp1_ref[...])
        prev2 = jnp.where(t >= 2, prev2, sp2_ref[...])
        u_ref[...] = u
    else:
        u_ref[0] = u[tm - hist:tm, :]
        ubuf[0:hist, :] = u[tm - hist:tm, :]
    cw = cw_ref[...]
    cy = cbias_ref[...] + prev2 * cw[0:1, :]
    cy = cy + prev1 * cw[1:2, :]
    cy = cy + u * cw[2:3, :]
    cb = _dot(xb, wm_ref[:, c1:c2])
    c = cb * cy
    a_d = _dot(a_ref[...].astype(MXU_DTYPE), wb0_ref[...])
    c_d = _dot(c.astype(MXU_DTYPE), wb1_ref[...])
    ga = _dot(xb, wm_ref[:, c3:c4])
    merged = jax.nn.sigmoid(ga) * a_d
    gc = _dot(xb, wm_ref[:, c4:])
    merged = merged + jax.nn.sigmoid(gc) * c_d
    mix = _dot(merged.astype(MXU_DTYPE), wo_ref[...])
    x1 = _layer_norm(alpha * x + mix, g_ref[...], b_ref[...])
    x1_ref[...] = x1
    gates_ref[...] = _router_gates(x1, wr_ref, rb_ref)


def _const_spec(shape, index):
    return pl.BlockSpec(shape, index, pipeline_mode=pl.Buffered(1))


def _mix(x, a, wm, conv_w, conv_b, wb0, wb1, wo, ln_g, ln_b, wr, rb, layer, tm, alpha, seq=None, sp1=None, sp2=None,
         t_new=None):
    n, d = x.shape
    sample = sp1 is not None
    row = lambda i: (i, 0)
    lay3 = lambda i: (layer, 0, 0)
    in_specs = [
        pl.BlockSpec((tm, d), row),
        pl.BlockSpec((tm, 512), row),
        _const_spec((None,) + wm.shape[1:], lay3),
        pl.BlockSpec((None, CONV_W, CONV_DIM), lay3),
        pl.BlockSpec((None, 1, CONV_DIM), lay3),
        _const_spec((None,) + wb0.shape[1:], lay3),
        _const_spec((None,) + wb1.shape[1:], lay3),
        _const_spec((None,) + wo.shape[1:], lay3),
        pl.BlockSpec((None, 1, d), lay3),
        pl.BlockSpec((None, 1, d), lay3),
        pl.BlockSpec(wr.shape, lambda i: (0, 0)),
        pl.BlockSpec(rb.shape, lambda i: (0, 0)),
    ]
    args = [x, a, wm, conv_w, conv_b, wb0, wb1, wo, ln_g, ln_b, wr, rb]
    if sample:
        in_specs += [pl.BlockSpec((None, tm, CONV_DIM), lambda i: (layer, i, 0))] * 2
        args += [sp1, sp2]
        u_shape = jax.ShapeDtypeStruct((n, CONV_DIM), F32)
        u_spec = pl.BlockSpec((tm, CONV_DIM), row)
        blocks_per_seq = 1
    else:
        blocks_per_seq = seq // tm
        u_shape = jax.ShapeDtypeStruct((n // seq, SUBLANES, CONV_DIM), F32)
        u_spec = pl.BlockSpec((1, SUBLANES, CONV_DIM), lambda i: (i // blocks_per_seq, 0, 0))
    return pl.pallas_call(
        functools.partial(_mix_kernel, alpha=alpha, sample=sample, t_new=t_new, blocks_per_seq=blocks_per_seq),
        grid=(n // tm,),
        in_specs=in_specs,
        out_specs=[pl.BlockSpec((tm, d), row), pl.BlockSpec((tm, LANES), row), u_spec],
        out_shape=[jax.ShapeDtypeStruct((n, d), F32), jax.ShapeDtypeStruct((n, LANES), F32), u_shape],
        scratch_shapes=[pltpu.VMEM((SUBLANES + tm, CONV_DIM), F32)],
        compiler_params=_params(1),
        name="mix_sample" if sample else "mix_prompt",
    )(*args)


def _moe_kernel(x_ref, gates_ref, wg_ref, wu_ref, wd_ref, g_ref, b_ref, o_ref, acc_ref, *, alpha):
    x1 = x_ref[...]
    xb = x1.astype(MXU_DTYPE)
    gates = gates_ref[...]
    for e in range(N_EXPERTS):
        h = jax.nn.silu(_dot(xb, wg_ref[e])) * _dot(xb, wu_ref[e])
        h = h * gates[:, e:e + 1]
        y = _dot(h.astype(MXU_DTYPE), wd_ref[e])
        if e == 0:
            acc_ref[...] = y
        else:
            acc_ref[...] += y
    o_ref[...] = _layer_norm(alpha * x1 + acc_ref[...], g_ref[...], b_ref[...])


def _moe(x1, gates, wg, wu, wd, ln_g, ln_b, layer, tm, alpha):
    n, d = x1.shape
    row = lambda i: (i, 0)
    lay4 = lambda i: (layer, 0, 0, 0)
    lay3 = lambda i: (layer, 0, 0)
    return pl.pallas_call(
        functools.partial(_moe_kernel, alpha=alpha),
        grid=(n // tm,),
        in_specs=[
            pl.BlockSpec((tm, d), row),
            pl.BlockSpec((tm, LANES), row),
            _const_spec((None,) + wg.shape[1:], lay4),
            _const_spec((None,) + wu.shape[1:], lay4),
            _const_spec((None,) + wd.shape[1:], lay4),
            pl.BlockSpec((None, 1, d), lay3),
            pl.BlockSpec((None, 1, d), lay3),
        ],
        out_specs=pl.BlockSpec((tm, d), row),
        out_shape=jax.ShapeDtypeStruct((n, d), F32),
        scratch_shapes=[pltpu.VMEM((tm, d), F32)],
        compiler_params=_params(1),
        name="moe",
    )(x1, gates, wg, wu, wd, ln_g, ln_b)


def _rope_tables(pos):
    half = HEAD_DIM // 2
    inv = 1.0 / (ROPE_THETA ** (jnp.arange(half, dtype=F32) / half))
    ang = pos.astype(F32)[:, None] * inv[None, :]
    cos = jnp.cos(ang)
    sin = jnp.sin(ang)
    reps = LANES // HEAD_DIM
    return jnp.tile(cos, (1, 2 * reps)), jnp.tile(jnp.concatenate([-sin, sin], axis=1), (1, reps))


def _block(n, target):
    b = min(n, target)
    assert n % b == 0, (n, target)
    return b


def kernel(x_prompt, x_sample, cache_k, cache_v, cache_ik, state_conv, page_table, w_in, conv_w, conv_b, w_branch, w_out, ln1_g, ln1_b, ln2_g, ln2_b, w_router, router_bias, w_exp_gate, w_exp_up, w_exp_down):
    depth, d_model, _ = w_in.shape
    batch, seq, _ = x_prompt.shape
    n_req, t_new, _ = x_sample.shape
    n_pages = page_table.shape[1]
    page = cache_k.shape[2]
    past = n_pages * page
    assert t_new >= CONV_W - 1 and t_new % SUBLANES == 0 and page % LANES == 0
    alpha = float((2 * depth) ** 0.25)
    k_prompt_sel = min(TOPK_MAX, seq // 4)
    k_sample_sel = min(TOPK_MAX, (past + t_new) // 4)

    o = np.concatenate([[0], np.cumsum(SPLIT_SIZES)])
    perm = np.asarray(HEAD_PERM)
    wq = w_in[:, :, o[0]:o[1]].reshape(depth, d_model, N_HEADS, HEAD_DIM)[:, :, perm].reshape(depth, d_model, -1)
    w_ik = w_in[:, :, o[4]:o[5]]
    w_iw = jnp.pad(w_in[:, :, o[5]:o[6]], ((0, 0), (0, 0), (0, LANES - N_IDX_HEADS)))
    wa = jnp.concatenate([wq, w_in[:, :, o[1]:o[4]], w_ik, w_ik, w_iw], axis=-1).astype(MXU_DTYPE)
    wiw_t = jnp.transpose(w_iw[:, :, :SUBLANES], (0, 2, 1)).astype(MXU_DTYPE)
    wm = w_in[:, :, o[6]:].astype(MXU_DTYPE)
    wb0 = w_branch[:, 0].reshape(depth, N_HEADS, HEAD_DIM, d_model)[:, perm].reshape(depth, -1, d_model)
    wb0 = wb0.astype(MXU_DTYPE)
    wb1 = w_branch[:, 1].astype(MXU_DTYPE)
    wo = w_out.astype(MXU_DTYPE)
    wr = jnp.pad(w_router.astype(F32), ((0, 0), (0, LANES - N_EXPERTS)))
    rb = jnp.pad(router_bias.astype(F32), (0, LANES - N_EXPERTS)).reshape(1, LANES)
    wg = w_exp_gate.astype(MXU_DTYPE)
    wu = w_exp_up.astype(MXU_DTYPE)
    wd = w_exp_down.astype(MXU_DTYPE)
    ln1g, ln1b, ln2g, ln2b = (a.reshape(depth, 1, d_model) for a in (ln1_g, ln1_b, ln2_g, ln2_b))
    cbias = conv_b.reshape(depth, 1, CONV_DIM)

    cos_p, sin_p = _rope_tables(jnp.arange(seq, dtype=jnp.int32))
    pos_s = jnp.tile(past + jnp.arange(t_new, dtype=jnp.int32), n_req)
    cos_s, sin_s = _rope_tables(pos_s)
    ck_t = jnp.transpose(cache_k, (0, 1, 3, 4, 2))
    cv_t = jnp.transpose(cache_v, (0, 1, 3, 4, 2))
    cik_t = jnp.transpose(cache_ik, (0, 1, 3, 2))
    zeros_state = jnp.zeros((depth, n_req, t_new, CONV_DIM), F32)
    sp1 = zeros_state.at[:, :, 0].set(state_conv[:, :, 1]).reshape(depth, n_req * t_new, CONV_DIM)
    sp2 = zeros_state.at[:, :, :2].set(state_conv).reshape(depth, n_req * t_new, CONV_DIM)

    qb = _block(seq, 256)
    tri = (np.arange(qb)[:, None] < np.arange(qb)[None, :])
    tril_p = jnp.asarray(tri.T, MXU_DTYPE)
    tri_s = jnp.asarray(tri[:LANES, :LANES], MXU_DTYPE)
    tm_p = _block(seq, 512)
    n_s = n_req * t_new
    tm_s = _block(n_s, 512)
    rps = 2 if n_req % 2 == 0 else 1

    xp = x_prompt.reshape(batch * seq, d_model)
    xs = x_sample.reshape(n_s, d_model)
    outs = [[] for _ in range(8)]
    for l in range(depth):
        q, k, v, iq, ik, kb, vt, ikb, iwt = _attn_proj(xp, wa, l, cos_p, sin_p, tm_p, wiw_t=wiw_t, chunk=qb)
        a = _prompt_attn(q, iq, iwt, kb, vt, ikb, tril_p, batch, seq, qb, k_prompt_sel)
        x1, gates, u_last = _mix(xp, a, wm, conv_w, cbias, wb0, wb1, wo, ln1g, ln1b, wr, rb, l, tm_p, alpha, seq=seq)
        xp = _moe(x1, gates, wg, wu, wd, ln2g, ln2b, l, tm_p, alpha)
        outs[0].append(k.reshape(batch, seq, N_KV_HEADS, HEAD_DIM))
        outs[1].append(v.reshape(batch, seq, N_KV_HEADS, HEAD_DIM))
        outs[2].append(ik.reshape(batch, seq, IDX_DIM))
        outs[3].append(u_last[:, SUBLANES - (CONV_W - 1):])

        q, k, v, iq, ik, iw = _attn_proj(xs, wa, l, cos_s, sin_s, tm_s)
        a = _sample_attn(page_table, q, iq, iw, k, v, ik, tri_s, ck_t, cv_t, cik_t, l, rps, t_new, k_sample_sel)
        x1, gates, u = _mix(xs, a, wm, conv_w, cbias, wb0, wb1, wo, ln1g, ln1b, wr, rb, l, tm_s, alpha,
                            sp1=sp1, sp2=sp2, t_new=t_new)
        xs = _moe(x1, gates, wg, wu, wd, ln2g, ln2b, l, tm_s, alpha)
        outs[4].append(k.reshape(n_req, t_new, N_KV_HEADS, HEAD_DIM))
        outs[5].append(v.reshape(n_req, t_new, N_KV_HEADS, HEAD_DIM))
        outs[6].append(ik.reshape(n_req, t_new, IDX_DIM))
        outs[7].append(u.reshape(n_req, t_new, CONV_DIM)[:, t_new - (CONV_W - 1):])
    stacked = [jnp.stack(o_) for o_ in outs]
    return (xp.reshape(batch, seq, d_model), xs.reshape(n_req, t_new, d_model), *stacked)
```

```python
import functools

import numpy as np
import jax
import jax.numpy as jnp
from jax import lax
from jax.experimental import pallas as pl
from jax.experimental.pallas import tpu as pltpu

N_HEADS = 8
HEAD_DIM = 64
N_KV_HEADS = 2
KV_GROUP = N_HEADS // N_KV_HEADS
N_IDX_HEADS = 4
IDX_DIM = 64
TOPK_MAX = 256
ROPE_THETA = 10000.0
CONV_DIM = 512
CONV_W = 3
N_EXPERTS = 16
N_GROUPS = 4
EXPERTS_PER_GROUP = N_EXPERTS // N_GROUPS
TOP_K_EXPERTS = 2
D_EXPERT = 256
LN_EPS = 1e-5
NEG_INF = -1e30
SPLIT_SIZES = (N_HEADS * HEAD_DIM, N_KV_HEADS * HEAD_DIM, N_KV_HEADS * HEAD_DIM, N_IDX_HEADS * IDX_DIM,
               IDX_DIM, N_IDX_HEADS, CONV_DIM, CONV_DIM, CONV_DIM)

LANES = 128
SUBLANES = 8
VMEM_LIMIT = 56 * 1024 * 1024
F32 = jnp.float32
MXU_DTYPE = jnp.bfloat16
F32_MIN_NORMAL = float(np.finfo(np.float32).tiny)
LOG2E = float(np.log2(np.e))
ELEMENTWISE_VREGS = 16
PROMPT_QUERY_BLOCK = 512
PROMPT_KEY_CHUNK = 256
HEAD_PERM = tuple(g * KV_GROUP + j for j in range(KV_GROUP) for g in range(N_KV_HEADS))

_NT = (((1,), (1,)), ((), ()))


def _dot(a, b):
    return jnp.dot(a, b, preferred_element_type=F32)


def _dot_nt(a, b):
    return lax.dot_general(a, b, _NT, preferred_element_type=F32)


def _params(n_grid):
    return pltpu.CompilerParams(dimension_semantics=("arbitrary",) * n_grid, vmem_limit_bytes=VMEM_LIMIT)


def _layer_norm(y, g, b):
    mu = jnp.mean(y, axis=-1, keepdims=True)
    yc = y - mu
    var = jnp.mean(yc * yc, axis=-1, keepdims=True)
    return yc * lax.rsqrt(var + LN_EPS) * g + b


def _attn_proj_kernel(*refs, prompt):
    if prompt:
        (x_ref, w_ref, cos_ref, sin_ref, wiw_ref,
         q_ref, k_ref, v_ref, iq_ref, ik_ref, kb_ref, vt_ref, ikb_ref, iwt_ref) = refs
    else:
        x_ref, w_ref, cos_ref, sin_ref, q_ref, k_ref, v_ref, iq_ref, ik_ref, iw_ref = refs
    tm = x_ref.shape[0]
    xb = x_ref[...].astype(MXU_DTYPE)
    cos = cos_ref[...]
    sin = sin_ref[...]
    lane = lax.broadcasted_iota(jnp.int32, (tm, LANES), 1)
    first_half = (lane & (HEAD_DIM // 2)) == 0

    def proj(col):
        return _dot(xb, w_ref[:, col:col + LANES])

    def rope(g):
        partner = jnp.where(first_half, pltpu.roll(g, LANES - HEAD_DIM // 2, 1), pltpu.roll(g, HEAD_DIM // 2, 1))
        return g * cos + partner * sin

    for j in range(4):
        q_ref[:, LANES * j:LANES * (j + 1)] = (rope(proj(LANES * j)) * (HEAD_DIM ** -0.5 * LOG2E)).astype(q_ref.dtype)
    k = rope(proj(512))
    k_ref[...] = k
    v = proj(640)
    v_ref[...] = v
    for j in range(2):
        iq_ref[:, LANES * j:LANES * (j + 1)] = rope(proj(768 + LANES * j)).astype(iq_ref.dtype)
    ik2 = rope(proj(1024))
    ik_ref[...] = ik2[:, :IDX_DIM]
    iw_scale = N_IDX_HEADS ** -0.5 * IDX_DIM ** -0.5
    if prompt:
        kb_ref[...] = k.astype(kb_ref.dtype)
        ikb_ref[...] = ik2.astype(ikb_ref.dtype)
        chunk = vt_ref.shape[2]
        for j in range(tm // chunk):
            vt_ref[j] = v[chunk * j:chunk * (j + 1), :].T.astype(vt_ref.dtype)
        iwt_ref[...] = _dot_nt(wiw_ref[...], xb) * iw_scale
    else:
        iw_ref[...] = proj(1152) * iw_scale


def _attn_proj(x, wa, layer, cos, sin, tm, wiw_t=None, chunk=None):
    n, d = x.shape
    prompt = wiw_t is not None
    nblk_pos = cos.shape[0] // tm
    row = lambda i: (i, 0)
    pos = lambda i: (i % nblk_pos, 0)
    low = MXU_DTYPE if prompt else F32
    outs = [((n, 512), low, (tm, 512), row),
            ((n, LANES), F32, (tm, LANES), row),
            ((n, LANES), F32, (tm, LANES), row),
            ((n, 256), low, (tm, 256), row),
            ((n, IDX_DIM), F32, (tm, IDX_DIM), row)]
    in_specs = [
        pl.BlockSpec((tm, d), row),
        pl.BlockSpec((None, d, wa.shape[2]), lambda i: (layer, 0, 0)),
        pl.BlockSpec((tm, LANES), pos),
        pl.BlockSpec((tm, LANES), pos),
    ]
    args = [x, wa, cos, sin]
    if prompt:
        in_specs.append(pl.BlockSpec((None,) + wiw_t.shape[1:], lambda i: (layer, 0, 0)))
        args.append(wiw_t)
        outs += [((n, LANES), MXU_DTYPE, (tm, LANES), row),
                 ((n // chunk, LANES, chunk), MXU_DTYPE, (tm // chunk, LANES, chunk), lambda i: (i, 0, 0)),
                 ((n, LANES), MXU_DTYPE, (tm, LANES), row),
                 ((SUBLANES, n), F32, (SUBLANES, tm), lambda i: (0, i))]
    else:
        outs.append(((n, LANES), F32, (tm, LANES), row))
    return pl.pallas_call(
        functools.partial(_attn_proj_kernel, prompt=prompt),
        grid=(n // tm,),
        in_specs=in_specs,
        out_specs=[pl.BlockSpec(blk, idx) for _, _, blk, idx in outs],
        out_shape=[jax.ShapeDtypeStruct(shape, dt) for shape, dt, _, _ in outs],
        compiler_params=_params(1),
        name="attn_proj_prompt" if prompt else "attn_proj_sample",
    )(*args)


def _ordered_key_to_float(u):
    int_min = jnp.int32(-2 ** 31)
    ck = u ^ int_min
    bits = jnp.where(ck >= 0, ck, ck ^ jnp.int32(2 ** 31 - 1))
    return lax.bitcast_convert_type(bits, F32)


def _bisect(count_ge, u, cnt_u, first_bit, n_bits, k_sel):
    def bit_body(it, carry):
        u, cnt_u = carry
        bit = lax.shift_left(jnp.int32(1), jnp.int32(first_bit) - it)
        cu = u | bit
        cnt = count_ge(_ordered_key_to_float(cu))
        take = cnt >= k_sel
        return jnp.where(take, cu, u), jnp.where(take, cnt, cnt_u)

    return lax.fori_loop(0, n_bits, bit_body, (u, cnt_u))


def _bisect_start(rows):
    return jnp.zeros((rows, 1), jnp.int32), jnp.full((rows, 1), 3e38, F32)


def _threshold_search(count_ge, rows, k_sel):
    u, cnt = _bisect(count_ge, *_bisect_start(rows), 31, 32, k_sel)
    return _ordered_key_to_float(u), cnt


def _floor_to_bf16(x):
    b = lax.bitcast_convert_type(x, jnp.int32)
    away = lax.shift_right_arithmetic(b, jnp.int32(31)) & jnp.int32(0xFFFF)
    return lax.bitcast_convert_type((b + away) & jnp.int32(-65536), F32).astype(jnp.bfloat16)


def _indexer_scores(dots, wcol, zval):
    s = None
    for d, w in zip(dots, wcol):
        t = jnp.maximum(d, 0.0) * w
        s = t if s is None else s + t
    return jnp.where(s == 0.0, zval, s)


def _zero_value(wcol):
    wmax = functools.reduce(jnp.maximum, wcol)
    return jnp.where(wmax < 0.0, -F32_MIN_NORMAL, 0.0).astype(F32)


def _prompt_attn_kernel(q_ref, iq_ref, iwt_ref, kb_ref, vt_ref, ikb_ref, tril_ref, a_ref,
                        s_scr, hi_scr, qpad_scr, iqp_scr, bias_scr, tie_scr, lg_scr, p_scr, m_scr, acc_scr,
                        *, k_sel):
    qb = q_ref.shape[0]
    kc = tril_ref.shape[0]
    n_diag = qb // kc
    rb = ELEMENTWISE_VREGS * SUBLANES * LANES // qb
    i = pl.program_id(1)
    n_full = i * n_diag
    kf = float(k_sel)
    lane = lax.broadcasted_iota(jnp.int32, (qb, LANES), 1)
    lo_half = lane < HEAD_DIM
    halves = (lo_half, jnp.logical_not(lo_half))
    tile = lambda t: slice(LANES * t, LANES * (t + 1))
    blocks = [slice(r, r + rb) for r in range(0, kc, rb)]

    def causal(rows, d):
        key = kc * d + rows.start + lax.broadcasted_iota(jnp.int32, (rb, qb), 0)
        return key <= lax.broadcasted_iota(jnp.int32, (rb, qb), 1)

    for h in range(N_IDX_HEADS):
        iqp_scr[h] = jnp.where(halves[h % 2], iq_ref[:, tile(h // 2)], jnp.zeros((), iq_ref.dtype))
    for hh in range(N_HEADS):
        g, j = divmod(hh, KV_GROUP)
        qpad_scr[hh] = jnp.where(halves[g], q_ref[:, tile(j)], jnp.zeros((), q_ref.dtype))
    iwt = iwt_ref[...]
    wrow = [iwt[h:h + 1, :] for h in range(N_IDX_HEADS)]
    zval = _zero_value(wrow)

    def key_rows(ref, c):
        return ref[pl.ds(pl.multiple_of(c * kc, kc), kc), :]

    def score_chunk(c, diag):
        ikc = key_rows(ikb_ref, c)
        for h in range(N_IDX_HEADS):
            lg_scr[h] = _dot_nt(ikc, iqp_scr[h])
        for rows in blocks:
            s = None
            for h in range(N_IDX_HEADS):
                term = jnp.maximum(lg_scr[h, rows, :], 0.0) * wrow[h]
                s = term if s is None else s + term
            s = jnp.where(s == 0.0, zval, s)
            if diag is not None:
                s = jnp.where(causal(rows, diag), s, NEG_INF)
            s_scr[c, rows, :] = s
            hi_scr[c, rows, :] = _floor_to_bf16(s)

    def score_body(c, carry):
        score_chunk(c, None)
        return carry

    lax.fori_loop(0, n_full, score_body, 0)
    for d in range(n_diag):
        score_chunk(n_full + d, d)

    def count(scr, thr_row, one, zero, strict=False):
        thr_b = jnp.broadcast_to(thr_row, (rb, qb))

        def body(c, acc):
            for rows in blocks:
                x = scr[c, rows, :]
                acc = acc + jnp.where(x > thr_b if strict else x >= thr_b, one, zero)
            return acc

        acc = lax.fori_loop(0, n_full + n_diag, body, jnp.zeros((rb, qb), one.dtype))
        return jnp.sum(acc.astype(F32), axis=0, keepdims=True)

    one_h, zero_h = jnp.ones((), hi_scr.dtype), jnp.zeros((), hi_scr.dtype)
    one_f, zero_f = jnp.ones((), F32), jnp.zeros((), F32)
    qpos = i * qb + lax.broadcasted_iota(jnp.int32, (1, qb), 1)
    short = qpos < k_sel

    def unresolved(cnt):
        return jnp.max(jnp.where(jnp.logical_or(short, cnt == kf), 0, 1)).astype(jnp.int32)

    u, cnt = _bisect(lambda t: count(hi_scr, t.astype(hi_scr.dtype), one_h, zero_h),
                     jnp.zeros((1, qb), jnp.int32), jnp.full((1, qb), 3e38, F32), 31, 16, kf)

    def low_bits_body(carry):
        it, u, cnt, _ = carry
        for _ in range(2):
            cu = u | lax.shift_left(jnp.int32(1), jnp.int32(15) - it)
            c = count(s_scr, _ordered_key_to_float(cu), one_f, zero_f)
            take = c >= kf
            u, cnt, it = jnp.where(take, cu, u), jnp.where(take, c, cnt), it + 1
        return it, u, cnt, unresolved(cnt)

    _, u, cnt_ge, _ = lax.while_loop(lambda cr: jnp.logical_and(cr[0] < 16, cr[3] > 0), low_bits_body,
                                     (jnp.int32(0), u, cnt, unresolved(cnt)))
    thr = jnp.where(short, -jnp.inf, _ordered_key_to_float(u))
    need = kf - count(s_scr, thr, one_f, zero_f, strict=True)
    has_tie = jnp.logical_and(jnp.logical_not(short), cnt_ge > kf)
    any_tie = jnp.max(jnp.where(has_tie, 1.0, 0.0)) > 0.5

    def attn_chunk(c, tie_seen, diag, ties):
        if ties:
            eq = jnp.where(s_scr[c] == thr, one_f, zero_f)
            p_scr[0] = eq.astype(p_scr.dtype)
            tie_scr[...] = _dot(tril_ref[...], p_scr[0])
        for rows in blocks:
            s = s_scr[c, rows, :]
            if ties:
                first = tie_seen + tie_scr[rows, :] < need
                sel = jnp.logical_or(s > thr, jnp.logical_and(s == thr, first))
            else:
                sel = s >= thr
            if diag is not None:
                sel = jnp.logical_and(sel, causal(rows, diag))
            bias_scr[rows, :] = jnp.where(sel, 0.0, NEG_INF)
        if ties:
            tie_seen = tie_seen + jnp.sum(eq, axis=0, keepdims=True)
        vt = vt_ref[c]
        top = lax.broadcasted_iota(jnp.int32, vt.shape, 0) < HEAD_DIM
        one_v = jnp.ones((), vt.dtype)
        vt_aug = (jnp.where(top, vt, one_v), jnp.where(top, one_v, vt))
        k_chunk = key_rows(kb_ref, c)
        for hh in range(N_HEADS):
            lg_scr[hh] = _dot_nt(k_chunk, qpad_scr[hh])
        alpha = []
        for hh in range(N_HEADS):
            top_x = None
            for rows in blocks:
                x = lg_scr[hh, rows, :] + bias_scr[rows, :]
                top_x = x if top_x is None else jnp.maximum(top_x, x)
            m_old = m_scr[hh]
            m_new = jnp.maximum(m_old, jnp.max(top_x, axis=0, keepdims=True))
            for rows in blocks:
                x = lg_scr[hh, rows, :] + bias_scr[rows, :]
                p_scr[hh, rows, :] = jnp.exp2(x - m_new).astype(p_scr.dtype)
            alpha.append(jnp.exp2(m_old - m_new))
            m_scr[hh] = m_new
        for hh in range(N_HEADS):
            acc_scr[hh] = acc_scr[hh] * alpha[hh] + _dot(vt_aug[hh // KV_GROUP], p_scr[hh])
        return tie_seen

    def run(ties):
        acc_scr[...] = jnp.zeros_like(acc_scr)
        m_scr[...] = jnp.full_like(m_scr, NEG_INF)
        seen = lax.fori_loop(0, n_full, lambda c, ts: attn_chunk(c, ts, None, ties), jnp.zeros((1, qb), F32))
        for d in range(n_diag):
            seen = attn_chunk(n_full + d, seen, d, ties)
        for j in range(KV_GROUP):
            a0 = acc_scr[j]
            a1 = acc_scr[KV_GROUP + j]
            o0 = a0[:HEAD_DIM] / a0[HEAD_DIM:HEAD_DIM + 1]
            o1 = a1[HEAD_DIM:] / a1[0:1]
            a_ref[:, tile(j)] = jnp.concatenate([o0, o1], axis=0).T.astype(a_ref.dtype)

    @pl.when(any_tie)
    def _():
        run(True)

    @pl.when(jnp.logical_not(any_tie))
    def _():
        run(False)


def _prompt_attn(q, iq, iwt, kb, vt, ikb, tril, batch, seq, qb, kc, k_sel):
    n = q.shape[0]
    nq = seq // qb
    nk = seq // kc
    rb = ELEMENTWISE_VREGS * SUBLANES * LANES // qb
    assert nk * (kc // rb) <= 256 and kc % rb == 0 and rb % 16 == 0 and qb % kc == 0
    qrow = lambda b, i: (b * nq + i, 0)
    keys = lambda b, i: (b, 0)
    once = dict(pipeline_mode=pl.Buffered(1))
    return pl.pallas_call(
        functools.partial(_prompt_attn_kernel, k_sel=k_sel),
        grid=(batch, nq),
        in_specs=[
            pl.BlockSpec((qb, 512), qrow),
            pl.BlockSpec((qb, 256), qrow),
            pl.BlockSpec((SUBLANES, qb), lambda b, i: (0, b * nq + i)),
            pl.BlockSpec((seq, LANES), keys, **once),
            pl.BlockSpec((nk, LANES, kc), lambda b, i: (b, 0, 0), **once),
            pl.BlockSpec((seq, LANES), keys, **once),
            pl.BlockSpec((kc, kc), lambda b, i: (0, 0)),
        ],
        out_specs=pl.BlockSpec((qb, 512), qrow),
        out_shape=jax.ShapeDtypeStruct((n, 512), MXU_DTYPE),
        scratch_shapes=[
            pltpu.VMEM((nk, kc, qb), F32),
            pltpu.VMEM((nk, kc, qb), jnp.bfloat16),
            pltpu.VMEM((N_HEADS, qb, LANES), MXU_DTYPE),
            pltpu.VMEM((N_IDX_HEADS, qb, LANES), MXU_DTYPE),
            pltpu.VMEM((kc, qb), F32),
            pltpu.VMEM((kc, qb), F32),
            pltpu.VMEM((N_HEADS, kc, qb), F32),
            pltpu.VMEM((N_HEADS, kc, qb), MXU_DTYPE),
            pltpu.VMEM((N_HEADS, 1, qb), F32),
            pltpu.VMEM((N_HEADS, LANES, qb), F32),
        ],
        compiler_params=_params(2),
        name="prompt_attn",
    )(q, iq, iwt, kb, vt, ikb, tril)


def _sample_attn_kernel(pt_ref, q_ref, iq_ref, iw_ref, k_ref, v_ref, ik_ref, tri_ref, ck_ref, cv_ref, cik_ref,
                        a_ref, kt_buf, vt_buf, ikt_buf, sems, *, layer, rps, t_new, n_pages, page, k_sel):
    step = pl.program_id(0)
    n_steps = pl.num_programs(0)
    slot = step % 2
    past = n_pages * page
    n_tile = past // LANES

    def page_copies(st, sl):
        copies = []
        for r in range(rps):
            for p in range(n_pages):
                phys = pt_ref[st * rps + r, p]
                dst = pl.ds(p * page, page)
                copies.append(pltpu.make_async_copy(ck_ref.at[layer, phys], kt_buf.at[sl, r, :, :, dst], sems.at[sl, 0]))
                copies.append(pltpu.make_async_copy(cv_ref.at[layer, phys], vt_buf.at[sl, r, :, :, dst], sems.at[sl, 1]))
                copies.append(pltpu.make_async_copy(cik_ref.at[layer, phys], ikt_buf.at[sl, r, :, dst], sems.at[sl, 2]))
        return copies

    @pl.when(step == 0)
    def _():
        for cp in page_copies(0, 0):
            cp.start()

    @pl.when(step + 1 < n_steps)
    def _():
        for cp in page_copies(step + 1, 1 - slot):
            cp.start()

    for cp in page_copies(step, slot):
        cp.wait()

    n_row = rps * t_new
    trow = lax.broadcasted_iota(jnp.int32, (n_row, LANES), 0) % t_new
    tcol = lax.broadcasted_iota(jnp.int32, (n_row, LANES), 1)
    causal_new = tcol <= trow
    tri = tri_ref[...]
    qpos = past + lax.broadcasted_iota(jnp.int32, (n_row, 1), 0) % t_new
    short = qpos < k_sel

    def pad_keys(x):
        return jnp.concatenate([x, jnp.zeros((LANES - t_new, x.shape[1]), x.dtype)], axis=0).astype(MXU_DTYPE)

    s_past_l, s_new_l = [], []
    for r in range(rps):
        rows = slice(r * t_new, (r + 1) * t_new)
        iq = iq_ref[rows, :]
        iq4 = jnp.concatenate([iq[:, IDX_DIM * h:IDX_DIM * (h + 1)] for h in range(N_IDX_HEADS)], axis=0)
        iq4 = iq4.astype(MXU_DTYPE)
        iw = iw_ref[rows, :]
        wcol = [iw[:, h:h + 1] for h in range(N_IDX_HEADS)]
        zval = _zero_value(wcol)
        d_past = _dot(iq4, ikt_buf[slot, r].astype(MXU_DTYPE))
        d_new = _dot_nt(iq4, pad_keys(ik_ref[rows, :]))
        split = lambda d: [d[t_new * h:t_new * (h + 1)] for h in range(N_IDX_HEADS)]
        s_past_l.append(_indexer_scores(split(d_past), wcol, zval))
        s_new_l.append(_indexer_scores(split(d_new), wcol, zval))
    s_past = jnp.concatenate(s_past_l, axis=0)
    s_new = jnp.where(causal_new, jnp.concatenate(s_new_l, axis=0), NEG_INF)

    def count(pred):
        return (jnp.sum(jnp.where(pred(s_past), 1.0, 0.0), axis=1, keepdims=True)
                + jnp.sum(jnp.where(pred(s_new), 1.0, 0.0), axis=1, keepdims=True))

    thr, _ = _threshold_search(lambda t: count(lambda s: s >= t), n_row, float(k_sel))
    thr = jnp.where(short, -jnp.inf, thr)
    need = float(k_sel) - count(lambda s: s > thr)

    eq_past = jnp.where(s_past == thr, 1.0, 0.0)
    eq_tiles = [eq_past[:, LANES * c:LANES * (c + 1)] for c in range(n_tile)]
    before = _dot(jnp.concatenate(eq_tiles, axis=0).astype(tri.dtype), tri)
    seen = jnp.zeros((n_row, 1), F32)
    sel_tiles = []
    for c in range(n_tile):
        sc = s_past[:, LANES * c:LANES * (c + 1)]
        rank = seen + before[n_row * c:n_row * (c + 1)]
        sel_tiles.append(jnp.where(jnp.logical_or(sc > thr, jnp.logical_and(sc == thr, rank < need)), 0.0, NEG_INF))
        seen = seen + jnp.sum(eq_tiles[c], axis=1, keepdims=True)
    bias_past_all = jnp.concatenate(sel_tiles, axis=1)
    eq_new = jnp.where(s_new == thr, 1.0, 0.0)
    rank_new = seen + _dot(eq_new.astype(tri.dtype), tri)
    sel_new_all = jnp.logical_or(s_new > thr, jnp.logical_and(s_new == thr, rank_new < need))
    bias_new_all = jnp.where(jnp.logical_and(sel_new_all, causal_new), 0.0, NEG_INF)

    for r in range(rps):
        rows = slice(r * t_new, (r + 1) * t_new)
        bias_past = bias_past_all[rows]
        sel_new = bias_new_all[rows] == 0.0
        q = q_ref[rows, :]
        for g in range(N_KV_HEADS):
            lo = HEAD_DIM * g
            qg = jnp.concatenate([q[:, LANES * j + lo:LANES * j + lo + HEAD_DIM] for j in range(KV_GROUP)], axis=0)
            qg = qg.astype(MXU_DTYPE)
            lp = _dot(qg, kt_buf[slot, r, g].astype(MXU_DTYPE))
            ln = _dot_nt(qg, pad_keys(k_ref[rows, lo:lo + HEAD_DIM]))
            vt = vt_buf[slot, r, g].astype(MXU_DTYPE)
            vn = pad_keys(v_ref[rows, lo:lo + HEAD_DIM])
            for j in range(KV_GROUP):
                hs = slice(t_new * j, t_new * (j + 1))
                lpj = jnp.where(bias_past == 0.0, lp[hs], NEG_INF)
                lnj = jnp.where(sel_new, ln[hs], NEG_INF)
                mx = jnp.maximum(jnp.max(lpj, axis=1, keepdims=True), jnp.max(lnj, axis=1, keepdims=True))
                pp = jnp.exp2(lpj - mx)
                pn = jnp.exp2(lnj - mx)
                den = jnp.sum(pp, axis=1, keepdims=True) + jnp.sum(pn, axis=1, keepdims=True)
                o = _dot_nt(pp.astype(MXU_DTYPE), vt) + _dot(pn.astype(MXU_DTYPE), vn)
                a_ref[rows, LANES * j + lo:LANES * j + lo + HEAD_DIM] = (o / den).astype(a_ref.dtype)


def _sample_attn(page_table, q, iq, iw, k, v, ik, tri, ck_t, cv_t, cik_t, layer, rps, t_new, k_sel):
    n = q.shape[0]
    n_req, n_pages = page_table.shape
    page = ck_t.shape[-1]
    past = n_pages * page
    rows = rps * t_new
    blk = lambda w: pl.BlockSpec((rows, w), lambda s, pt: (s, 0))
    any_spec = pl.BlockSpec(memory_space=pl.ANY)
    return pl.pallas_call(
        functools.partial(_sample_attn_kernel, layer=layer, rps=rps, t_new=t_new, n_pages=n_pages, page=page,
                          k_sel=k_sel),
        grid_spec=pltpu.PrefetchScalarGridSpec(
            num_scalar_prefetch=1,
            grid=(n_req // rps,),
            in_specs=[blk(512), blk(256), blk(LANES), blk(LANES), blk(LANES), blk(IDX_DIM),
                      pl.BlockSpec((LANES, LANES), lambda s, pt: (0, 0)),
                      any_spec, any_spec, any_spec],
            out_specs=blk(512),
            scratch_shapes=[
                pltpu.VMEM((2, rps, N_KV_HEADS, HEAD_DIM, past), F32),
                pltpu.VMEM((2, rps, N_KV_HEADS, HEAD_DIM, past), F32),
                pltpu.VMEM((2, rps, IDX_DIM, past), F32),
                pltpu.SemaphoreType.DMA((2, 3)),
            ],
        ),
        out_shape=jax.ShapeDtypeStruct((n, 512), F32),
        compiler_params=_params(1),
        name="sample_attn",
    )(page_table, q, iq, iw, k, v, ik, tri, ck_t, cv_t, cik_t)


def _router_gates(x1, wr_ref, rb_ref):
    tm = x1.shape[0]
    logits = jnp.dot(x1, wr_ref[...], preferred_element_type=F32, precision=lax.Precision.HIGHEST)
    aff = jax.nn.sigmoid(logits)
    biased = aff + rb_ref[...]
    lane = lax.broadcasted_iota(jnp.int32, (tm, LANES), 1)
    member = lane % EXPERTS_PER_GROUP
    group = lane // EXPERTS_PER_GROUP

    def other_member(x, d):
        return jnp.where(member + d < EXPERTS_PER_GROUP, pltpu.roll(x, LANES - d, 1),
                         pltpu.roll(x, EXPERTS_PER_GROUP - d, 1))

    def other_group(x, d):
        sh = EXPERTS_PER_GROUP * d
        return jnp.where(group + d < N_GROUPS, pltpu.roll(x, LANES - sh, 1), pltpu.roll(x, N_EXPERTS - sh, 1))

    def beaten_count(x, other, pos, n):
        cnt = jnp.zeros(x.shape, F32)
        for d in range(1, n):
            o = other(x, d)
            wins = jnp.logical_or(o > x, jnp.logical_and(o == x, pos + d >= n))
            cnt = cnt + jnp.where(wins, 1.0, 0.0)
        return cnt

    def group_sum(x):
        tot = x
        for d in range(1, EXPERTS_PER_GROUP):
            tot = tot + other_member(x, d)
        return tot

    top2 = beaten_count(biased, other_member, member, EXPERTS_PER_GROUP) < float(TOP_K_EXPERTS)
    grp_score = group_sum(jnp.where(top2, biased, 0.0))
    best = beaten_count(grp_score, other_group, group, N_GROUPS) < 1.0
    sel = jnp.logical_and(jnp.logical_and(top2, best), lane < N_EXPERTS)
    den = group_sum(jnp.where(sel, aff, 0.0))
    return jnp.where(sel, aff / jnp.where(sel, den, 1.0), 0.0)


def _mix_kernel(*refs, alpha, sample, t_new, blocks_per_seq):
    if sample:
        (x_ref, a_ref, wm_ref, cw_ref, cbias_ref, wb0_ref, wb1_ref, wo_ref, g_ref, b_ref, wr_ref, rb_ref,
         sp1_ref, sp2_ref, x1_ref, gates_ref, u_ref, ubuf) = refs
    else:
        (x_ref, a_ref, wm_ref, cw_ref, cbias_ref, wb0_ref, wb1_ref, wo_ref, g_ref, b_ref, wr_ref, rb_ref,
         x1_ref, gates_ref, u_ref, ubuf) = refs
    tm = x_ref.shape[0]
    hist = SUBLANES
    x = x_ref[...]
    xb = x.astype(MXU_DTYPE)
    c0, c1, c2, c3, c4 = 0, CONV_DIM, 2 * CONV_DIM, 3 * CONV_DIM, 3 * CONV_DIM + x.shape[1]
    ch = _dot(xb, wm_ref[:, c0:c1])
    cc = _dot(xb, wm_ref[:, c2:c3])
    u = cc * ch
    if sample:
        ubuf[0:hist, :] = jnp.zeros((hist, CONV_DIM), F32)
    else:
        @pl.when(pl.program_id(0) % blocks_per_seq == 0)
        def _():
            ubuf[0:hist, :] = jnp.zeros((hist, CONV_DIM), F32)
    ubuf[hist:hist + tm, :] = u
    prev1 = ubuf[hist - 1:hist - 1 + tm, :]
    prev2 = ubuf[hist - 2:hist - 2 + tm, :]
    if sample:
        t = lax.broadcasted_iota(jnp.int32, (tm, 1), 0) % t_new
        prev1 = jnp.where(t >= 1, prev1, sp1_ref[...])
        prev2 = jnp.where(t >= 2, prev2, sp2_ref[...])
        u_ref[...] = u
    else:
        u_ref[0] = u[tm - hist:tm, :]
        ubuf[0:hist, :] = u[tm - hist:tm, :]
    cw = cw_ref[...]
    cy = cbias_ref[...] + prev2 * cw[0:1, :]
    cy = cy + prev1 * cw[1:2, :]
    cy = cy + u * cw[2:3, :]
    cb = _dot(xb, wm_ref[:, c1:c2])
    c = cb * cy
    a_d = _dot(a_ref[...].astype(MXU_DTYPE), wb0_ref[...])
    c_d = _dot(c.astype(MXU_DTYPE), wb1_ref[...])
    ga = _dot(xb, wm_ref[:, c3:c4])
    merged = jax.nn.sigmoid(ga) * a_d
    gc = _dot(xb, wm_ref[:, c4:])
    merged = merged + jax.nn.sigmoid(gc) * c_d
    mix = _dot(merged.astype(MXU_DTYPE), wo_ref[...])
    x1 = _layer_norm(alpha * x + mix, g_ref[...], b_ref[...])
    x1_ref[...] = x1
    gates_ref[...] = _router_gates(x1, wr_ref, rb_ref)


def _const_spec(shape, index):
    return pl.BlockSpec(shape, index, pipeline_mode=pl.Buffered(1))


def _mix(x, a, wm, conv_w, conv_b, wb0, wb1, wo, ln_g, ln_b, wr, rb, layer, tm, alpha, seq=None, sp1=None, sp2=None,
         t_new=None):
    n, d = x.shape
    sample = sp1 is not None
    row = lambda i: (i, 0)
    lay3 = lambda i: (layer, 0, 0)
    in_specs = [
        pl.BlockSpec((tm, d), row),
        pl.BlockSpec((tm, 512), row),
        _const_spec((None,) + wm.shape[1:], lay3),
        pl.BlockSpec((None, CONV_W, CONV_DIM), lay3),
        pl.BlockSpec((None, 1, CONV_DIM), lay3),
        _const_spec((None,) + wb0.shape[1:], lay3),
        _const_spec((None,) + wb1.shape[1:], lay3),
        _const_spec((None,) + wo.shape[1:], lay3),
        pl.BlockSpec((None, 1, d), lay3),
        pl.BlockSpec((None, 1, d), lay3),
        pl.BlockSpec(wr.shape, lambda i: (0, 0)),
        pl.BlockSpec(rb.shape, lambda i: (0, 0)),
    ]
    args = [x, a, wm, conv_w, conv_b, wb0, wb1, wo, ln_g, ln_b, wr, rb]
    if sample:
        in_specs += [pl.BlockSpec((None, tm, CONV_DIM), lambda i: (layer, i, 0))] * 2
        args += [sp1, sp2]
        u_shape = jax.ShapeDtypeStruct((n, CONV_DIM), F32)
        u_spec = pl.BlockSpec((tm, CONV_DIM), row)
        blocks_per_seq = 1
    else:
        blocks_per_seq = seq // tm
        u_shape = jax.ShapeDtypeStruct((n // seq, SUBLANES, CONV_DIM), F32)
        u_spec = pl.BlockSpec((1, SUBLANES, CONV_DIM), lambda i: (i // blocks_per_seq, 0, 0))
    return pl.pallas_call(
        functools.partial(_mix_kernel, alpha=alpha, sample=sample, t_new=t_new, blocks_per_seq=blocks_per_seq),
        grid=(n // tm,),
        in_specs=in_specs,
        out_specs=[pl.BlockSpec((tm, d), row), pl.BlockSpec((tm, LANES), row), u_spec],
        out_shape=[jax.ShapeDtypeStruct((n, d), F32), jax.ShapeDtypeStruct((n, LANES), F32), u_shape],
        scratch_shapes=[pltpu.VMEM((SUBLANES + tm, CONV_DIM), F32)],
        compiler_params=_params(1),
        name="mix_sample" if sample else "mix_prompt",
    )(*args)


def _moe_kernel(x_ref, gates_ref, wg_ref, wu_ref, wd_ref, g_ref, b_ref, o_ref, acc_ref, *, alpha):
    x1 = x_ref[...]
    xb = x1.astype(MXU_DTYPE)
    gates = gates_ref[...]
    for e in range(N_EXPERTS):
        h = jax.nn.silu(_dot(xb, wg_ref[e])) * _dot(xb, wu_ref[e])
        h = h * gates[:, e:e + 1]
        y = _dot(h.astype(MXU_DTYPE), wd_ref[e])
        if e == 0:
            acc_ref[...] = y
        else:
            acc_ref[...] += y
    o_ref[...] = _layer_norm(alpha * x1 + acc_ref[...], g_ref[...], b_ref[...])


def _moe(x1, gates, wg, wu, wd, ln_g, ln_b, layer, tm, alpha):
    n, d = x1.shape
    row = lambda i: (i, 0)
    lay4 = lambda i: (layer, 0, 0, 0)
    lay3 = lambda i: (layer, 0, 0)
    return pl.pallas_call(
        functools.partial(_moe_kernel, alpha=alpha),
        grid=(n // tm,),
        in_specs=[
            pl.BlockSpec((tm, d), row),
            pl.BlockSpec((tm, LANES), row),
            _const_spec((None,) + wg.shape[1:], lay4),
            _const_spec((None,) + wu.shape[1:], lay4),
            _const_spec((None,) + wd.shape[1:], lay4),
            pl.BlockSpec((None, 1, d), lay3),
            pl.BlockSpec((None, 1, d), lay3),
        ],
        out_specs=pl.BlockSpec((tm, d), row),
        out_shape=jax.ShapeDtypeStruct((n, d), F32),
        scratch_shapes=[pltpu.VMEM((tm, d), F32)],
        compiler_params=_params(1),
        name="moe",
    )(x1, gates, wg, wu, wd, ln_g, ln_b)


def _rope_tables(pos):
    half = HEAD_DIM // 2
    inv = 1.0 / (ROPE_THETA ** (jnp.arange(half, dtype=F32) / half))
    ang = pos.astype(F32)[:, None] * inv[None, :]
    cos = jnp.cos(ang)
    sin = jnp.sin(ang)
    reps = LANES // HEAD_DIM
    return jnp.tile(cos, (1, 2 * reps)), jnp.tile(jnp.concatenate([-sin, sin], axis=1), (1, reps))


def _block(n, target):
    b = min(n, target)
    assert n % b == 0, (n, target)
    return b


def kernel(x_prompt, x_sample, cache_k, cache_v, cache_ik, state_conv, page_table, w_in, conv_w, conv_b, w_branch, w_out, ln1_g, ln1_b, ln2_g, ln2_b, w_router, router_bias, w_exp_gate, w_exp_up, w_exp_down):
    depth, d_model, _ = w_in.shape
    batch, seq, _ = x_prompt.shape
    n_req, t_new, _ = x_sample.shape
    n_pages = page_table.shape[1]
    page = cache_k.shape[2]
    past = n_pages * page
    assert t_new >= CONV_W - 1 and t_new % SUBLANES == 0 and page % LANES == 0
    alpha = float((2 * depth) ** 0.25)
    k_prompt_sel = min(TOPK_MAX, seq // 4)
    k_sample_sel = min(TOPK_MAX, (past + t_new) // 4)

    o = np.concatenate([[0], np.cumsum(SPLIT_SIZES)])
    perm = np.asarray(HEAD_PERM)
    wq = w_in[:, :, o[0]:o[1]].reshape(depth, d_model, N_HEADS, HEAD_DIM)[:, :, perm].reshape(depth, d_model, -1)
    w_ik = w_in[:, :, o[4]:o[5]]
    w_iw = jnp.pad(w_in[:, :, o[5]:o[6]], ((0, 0), (0, 0), (0, LANES - N_IDX_HEADS)))
    wa = jnp.concatenate([wq, w_in[:, :, o[1]:o[4]], w_ik, w_ik, w_iw], axis=-1).astype(MXU_DTYPE)
    wiw_t = jnp.transpose(w_iw[:, :, :SUBLANES], (0, 2, 1)).astype(MXU_DTYPE)
    wm = w_in[:, :, o[6]:].astype(MXU_DTYPE)
    wb0 = w_branch[:, 0].reshape(depth, N_HEADS, HEAD_DIM, d_model)[:, perm].reshape(depth, -1, d_model)
    wb0 = wb0.astype(MXU_DTYPE)
    wb1 = w_branch[:, 1].astype(MXU_DTYPE)
    wo = w_out.astype(MXU_DTYPE)
    wr = jnp.pad(w_router.astype(F32), ((0, 0), (0, LANES - N_EXPERTS)))
    rb = jnp.pad(router_bias.astype(F32), (0, LANES - N_EXPERTS)).reshape(1, LANES)
    wg = w_exp_gate.astype(MXU_DTYPE)
    wu = w_exp_up.astype(MXU_DTYPE)
    wd = w_exp_down.astype(MXU_DTYPE)
    ln1g, ln1b, ln2g, ln2b = (a.reshape(depth, 1, d_model) for a in (ln1_g, ln1_b, ln2_g, ln2_b))
    cbias = conv_b.reshape(depth, 1, CONV_DIM)

    cos_p, sin_p = _rope_tables(jnp.arange(seq, dtype=jnp.int32))
    pos_s = jnp.tile(past + jnp.arange(t_new, dtype=jnp.int32), n_req)
    cos_s, sin_s = _rope_tables(pos_s)
    ck_t = jnp.transpose(cache_k, (0, 1, 3, 4, 2))
    cv_t = jnp.transpose(cache_v, (0, 1, 3, 4, 2))
    cik_t = jnp.transpose(cache_ik, (0, 1, 3, 2))
    zeros_state = jnp.zeros((depth, n_req, t_new, CONV_DIM), F32)
    sp1 = zeros_state.at[:, :, 0].set(state_conv[:, :, 1]).reshape(depth, n_req * t_new, CONV_DIM)
    sp2 = zeros_state.at[:, :, :2].set(state_conv).reshape(depth, n_req * t_new, CONV_DIM)

    qb = _block(seq, PROMPT_QUERY_BLOCK)
    kc = _block(qb, PROMPT_KEY_CHUNK)
    tri = (np.arange(kc)[:, None] < np.arange(kc)[None, :])
    tril_p = jnp.asarray(tri.T, MXU_DTYPE)
    tri_s = jnp.asarray(tri[:LANES, :LANES], MXU_DTYPE)
    tm_p = _block(seq, 512)
    n_s = n_req * t_new
    tm_s = _block(n_s, 512)
    rps = next(r for r in (4, 2, 1) if n_req % r == 0)

    xp = x_prompt.reshape(batch * seq, d_model)
    xs = x_sample.reshape(n_s, d_model)
    outs = [[] for _ in range(8)]
    for l in range(depth):
        q, k, v, iq, ik, kb, vt, ikb, iwt = _attn_proj(xp, wa, l, cos_p, sin_p, tm_p, wiw_t=wiw_t, chunk=kc)
        a = _prompt_attn(q, iq, iwt, kb, vt, ikb, tril_p, batch, seq, qb, kc, k_prompt_sel)
        x1, gates, u_last = _mix(xp, a, wm, conv_w, cbias, wb0, wb1, wo, ln1g, ln1b, wr, rb, l, tm_p, alpha, seq=seq)
        xp = _moe(x1, gates, wg, wu, wd, ln2g, ln2b, l, tm_p, alpha)
        outs[0].append(k.reshape(batch, seq, N_KV_HEADS, HEAD_DIM))
        outs[1].append(v.reshape(batch, seq, N_KV_HEADS, HEAD_DIM))
        outs[2].append(ik.reshape(batch, seq, IDX_DIM))
        outs[3].append(u_last[:, SUBLANES - (CONV_W - 1):])

        q, k, v, iq, ik, iw = _attn_proj(xs, wa, l, cos_s, sin_s, tm_s)
        a = _sample_attn(page_table, q, iq, iw, k, v, ik, tri_s, ck_t, cv_t, cik_t, l, rps, t_new, k_sample_sel)
        x1, gates, u = _mix(xs, a, wm, conv_w, cbias, wb0, wb1, wo, ln1g, ln1b, wr, rb, l, tm_s, alpha,
                            sp1=sp1, sp2=sp2, t_new=t_new)
        xs = _moe(x1, gates, wg, wu, wd, ln2g, ln2b, l, tm_s, alpha)
        outs[4].append(k.reshape(n_req, t_new, N_KV_HEADS, HEAD_DIM))
        outs[5].append(v.reshape(n_req, t_new, N_KV_HEADS, HEAD_DIM))
        outs[6].append(ik.reshape(n_req, t_new, IDX_DIM))
        outs[7].append(u.reshape(n_req, t_new, CONV_DIM)[:, t_new - (CONV_W - 1):])
    stacked = [jnp.stack(o_) for o_ in outs]
    return (xp.reshape(batch, seq, d_model), xs.reshape(n_req, t_new, d_model), *stacked)
```

```python
import functools

import numpy as np
import jax
import jax.numpy as jnp
from jax import lax
from jax.experimental import pallas as pl
from jax.experimental.pallas import tpu as pltpu

N_HEADS = 8
HEAD_DIM = 64
N_KV_HEADS = 2
KV_GROUP = N_HEADS // N_KV_HEADS
N_IDX_HEADS = 4
IDX_DIM = 64
TOPK_MAX = 256
ROPE_THETA = 10000.0
CONV_DIM = 512
CONV_W = 3
N_EXPERTS = 16
N_GROUPS = 4
EXPERTS_PER_GROUP = N_EXPERTS // N_GROUPS
TOP_K_EXPERTS = 2
D_EXPERT = 256
LN_EPS = 1e-5
NEG_INF = -1e30
SPLIT_SIZES = (N_HEADS * HEAD_DIM, N_KV_HEADS * HEAD_DIM, N_KV_HEADS * HEAD_DIM, N_IDX_HEADS * IDX_DIM,
               IDX_DIM, N_IDX_HEADS, CONV_DIM, CONV_DIM, CONV_DIM)

LANES = 128
SUBLANES = 8
VMEM_LIMIT = 56 * 1024 * 1024
F32 = jnp.float32
MXU_DTYPE = jnp.bfloat16
F32_MIN_NORMAL = float(np.finfo(np.float32).tiny)
LOG2E = float(np.log2(np.e))
ELEMENTWISE_VREGS = 16
PROMPT_QUERY_BLOCK = 512
PROMPT_KEY_CHUNK = 256
HEAD_PERM = tuple(g * KV_GROUP + j for j in range(KV_GROUP) for g in range(N_KV_HEADS))

_NT = (((1,), (1,)), ((), ()))


def _dot(a, b):
    return jnp.dot(a, b, preferred_element_type=F32)


def _dot_nt(a, b):
    return lax.dot_general(a, b, _NT, preferred_element_type=F32)


def _params(n_grid):
    return pltpu.CompilerParams(dimension_semantics=("arbitrary",) * n_grid, vmem_limit_bytes=VMEM_LIMIT)


def _layer_norm(y, g, b):
    mu = jnp.mean(y, axis=-1, keepdims=True)
    yc = y - mu
    var = jnp.mean(yc * yc, axis=-1, keepdims=True)
    return yc * lax.rsqrt(var + LN_EPS) * g + b


def _attn_proj_kernel(*refs, prompt):
    if prompt:
        (x_ref, w_ref, cos_ref, sin_ref, wiw_ref,
         q_ref, k_ref, v_ref, iq_ref, ik_ref, kb_ref, vt_ref, ikb_ref, iwt_ref) = refs
    else:
        x_ref, w_ref, cos_ref, sin_ref, q_ref, k_ref, v_ref, iq_ref, ik_ref, iw_ref = refs
    tm = x_ref.shape[0]
    xb = x_ref[...].astype(MXU_DTYPE)
    cos = cos_ref[...]
    sin = sin_ref[...]
    lane = lax.broadcasted_iota(jnp.int32, (tm, LANES), 1)
    first_half = (lane & (HEAD_DIM // 2)) == 0

    def proj(col):
        return _dot(xb, w_ref[:, col:col + LANES])

    def rope(g):
        partner = jnp.where(first_half, pltpu.roll(g, LANES - HEAD_DIM // 2, 1), pltpu.roll(g, HEAD_DIM // 2, 1))
        return g * cos + partner * sin

    for j in range(4):
        q_ref[:, LANES * j:LANES * (j + 1)] = (rope(proj(LANES * j)) * (HEAD_DIM ** -0.5 * LOG2E)).astype(q_ref.dtype)
    k = rope(proj(512))
    k_ref[...] = k
    v = proj(640)
    v_ref[...] = v
    for j in range(2):
        iq_ref[:, LANES * j:LANES * (j + 1)] = rope(proj(768 + LANES * j)).astype(iq_ref.dtype)
    ik2 = rope(proj(1024))
    ik_ref[...] = ik2[:, :IDX_DIM]
    iw_scale = N_IDX_HEADS ** -0.5 * IDX_DIM ** -0.5
    if prompt:
        kb_ref[...] = k.astype(kb_ref.dtype)
        ikb_ref[...] = ik2.astype(ikb_ref.dtype)
        chunk = vt_ref.shape[2]
        for j in range(tm // chunk):
            vt_ref[j] = v[chunk * j:chunk * (j + 1), :].T.astype(vt_ref.dtype)
        iwt_ref[...] = _dot_nt(wiw_ref[...], xb) * iw_scale
    else:
        iw_ref[...] = proj(1152) * iw_scale


def _attn_proj(x, wa, layer, cos, sin, tm, wiw_t=None, chunk=None):
    n, d = x.shape
    prompt = wiw_t is not None
    nblk_pos = cos.shape[0] // tm
    row = lambda i: (i, 0)
    pos = lambda i: (i % nblk_pos, 0)
    low = MXU_DTYPE if prompt else F32
    outs = [((n, 512), low, (tm, 512), row),
            ((n, LANES), F32, (tm, LANES), row),
            ((n, LANES), F32, (tm, LANES), row),
            ((n, 256), low, (tm, 256), row),
            ((n, IDX_DIM), F32, (tm, IDX_DIM), row)]
    in_specs = [
        pl.BlockSpec((tm, d), row),
        pl.BlockSpec((None, d, wa.shape[2]), lambda i: (layer, 0, 0)),
        pl.BlockSpec((tm, LANES), pos),
        pl.BlockSpec((tm, LANES), pos),
    ]
    args = [x, wa, cos, sin]
    if prompt:
        in_specs.append(pl.BlockSpec((None,) + wiw_t.shape[1:], lambda i: (layer, 0, 0)))
        args.append(wiw_t)
        outs += [((n, LANES), MXU_DTYPE, (tm, LANES), row),
                 ((n // chunk, LANES, chunk), MXU_DTYPE, (tm // chunk, LANES, chunk), lambda i: (i, 0, 0)),
                 ((n, LANES), MXU_DTYPE, (tm, LANES), row),
                 ((SUBLANES, n), F32, (SUBLANES, tm), lambda i: (0, i))]
    else:
        outs.append(((n, LANES), F32, (tm, LANES), row))
    return pl.pallas_call(
        functools.partial(_attn_proj_kernel, prompt=prompt),
        grid=(n // tm,),
        in_specs=in_specs,
        out_specs=[pl.BlockSpec(blk, idx) for _, _, blk, idx in outs],
        out_shape=[jax.ShapeDtypeStruct(shape, dt) for shape, dt, _, _ in outs],
        compiler_params=_params(1),
        name="attn_proj_prompt" if prompt else "attn_proj_sample",
    )(*args)


def _ordered_key_to_float(u):
    int_min = jnp.int32(-2 ** 31)
    ck = u ^ int_min
    bits = jnp.where(ck >= 0, ck, ck ^ jnp.int32(2 ** 31 - 1))
    return lax.bitcast_convert_type(bits, F32)


def _bisect(count_ge, u, cnt_u, first_bit, n_bits, k_sel):
    def bit_body(it, carry):
        u, cnt_u = carry
        bit = lax.shift_left(jnp.int32(1), jnp.int32(first_bit) - it)
        cu = u | bit
        cnt = count_ge(_ordered_key_to_float(cu))
        take = cnt >= k_sel
        return jnp.where(take, cu, u), jnp.where(take, cnt, cnt_u)

    return lax.fori_loop(0, n_bits, bit_body, (u, cnt_u))


def _bisect_start(rows):
    return jnp.zeros((rows, 1), jnp.int32), jnp.full((rows, 1), 3e38, F32)


def _threshold_search(count_ge, rows, k_sel):
    u, cnt = _bisect(count_ge, *_bisect_start(rows), 31, 32, k_sel)
    return _ordered_key_to_float(u), cnt


def _floor_to_bf16(x):
    b = lax.bitcast_convert_type(x, jnp.int32)
    away = lax.shift_right_arithmetic(b, jnp.int32(31)) & jnp.int32(0xFFFF)
    return lax.bitcast_convert_type((b + away) & jnp.int32(-65536), F32).astype(jnp.bfloat16)


def _indexer_scores(dots, wcol, zval):
    s = None
    for d, w in zip(dots, wcol):
        t = jnp.maximum(d, 0.0) * w
        s = t if s is None else s + t
    return jnp.where(s == 0.0, zval, s)


def _zero_value(wcol):
    wmax = functools.reduce(jnp.maximum, wcol)
    return jnp.where(wmax < 0.0, -F32_MIN_NORMAL, 0.0).astype(F32)


def _prompt_attn_kernel(q_ref, iq_ref, iwt_ref, kb_ref, vt_ref, ikb_ref, tril_ref, a_ref,
                        s_scr, hi_scr, qpad_scr, iqp_scr, bias_scr, lg_scr, p_scr, m_scr, acc_scr,
                        *, k_sel):
    qb = q_ref.shape[0]
    kc = tril_ref.shape[0]
    n_diag = qb // kc
    rb = ELEMENTWISE_VREGS * SUBLANES * LANES // qb
    i = pl.program_id(1)
    n_full = i * n_diag
    kf = float(k_sel)
    lane = lax.broadcasted_iota(jnp.int32, (qb, LANES), 1)
    lo_half = lane < HEAD_DIM
    halves = (lo_half, jnp.logical_not(lo_half))
    tile = lambda t: slice(LANES * t, LANES * (t + 1))
    blocks = [slice(r, r + rb) for r in range(0, kc, rb)]

    def causal(rows, d):
        key = kc * d + rows.start + lax.broadcasted_iota(jnp.int32, (rb, qb), 0)
        return key <= lax.broadcasted_iota(jnp.int32, (rb, qb), 1)

    for h in range(N_IDX_HEADS):
        iqp_scr[h] = jnp.where(halves[h % 2], iq_ref[:, tile(h // 2)], jnp.zeros((), iq_ref.dtype))
    for hh in range(N_HEADS):
        g, j = divmod(hh, KV_GROUP)
        qpad_scr[hh] = jnp.where(halves[g], q_ref[:, tile(j)], jnp.zeros((), q_ref.dtype))
    iwt = iwt_ref[...]
    wrow = [iwt[h:h + 1, :] for h in range(N_IDX_HEADS)]
    zval = _zero_value(wrow)

    def key_rows(ref, c):
        return ref[pl.ds(pl.multiple_of(c * kc, kc), kc), :]

    def score_chunk(c, diag):
        ikc = key_rows(ikb_ref, c)
        for h in range(N_IDX_HEADS):
            lg_scr[h] = _dot_nt(ikc, iqp_scr[h])
        for rows in blocks:
            s = None
            for h in range(N_IDX_HEADS):
                term = jnp.maximum(lg_scr[h, rows, :], 0.0) * wrow[h]
                s = term if s is None else s + term
            s = jnp.where(s == 0.0, zval, s)
            if diag is not None:
                s = jnp.where(causal(rows, diag), s, NEG_INF)
            s_scr[c, rows, :] = s
            hi_scr[c, rows, :] = _floor_to_bf16(s)

    def score_body(c, carry):
        score_chunk(c, None)
        return carry

    lax.fori_loop(0, n_full, score_body, 0)
    for d in range(n_diag):
        score_chunk(n_full + d, d)

    def count(scr, thr_row, one, zero, strict=False):
        thr_b = jnp.broadcast_to(thr_row, (rb, qb))

        def body(c, acc):
            for rows in blocks:
                x = scr[c, rows, :]
                acc = acc + jnp.where(x > thr_b if strict else x >= thr_b, one, zero)
            return acc

        acc = lax.fori_loop(0, n_full + n_diag, body, jnp.zeros((rb, qb), one.dtype))
        return jnp.sum(acc.astype(F32), axis=0, keepdims=True)

    one_h, zero_h = jnp.ones((), hi_scr.dtype), jnp.zeros((), hi_scr.dtype)
    one_f, zero_f = jnp.ones((), F32), jnp.zeros((), F32)
    qpos = i * qb + lax.broadcasted_iota(jnp.int32, (1, qb), 1)
    short = qpos < k_sel

    def unresolved(cnt):
        return jnp.max(jnp.where(jnp.logical_or(short, cnt == kf), 0, 1)).astype(jnp.int32)

    u, cnt = _bisect(lambda t: count(hi_scr, t.astype(hi_scr.dtype), one_h, zero_h),
                     jnp.zeros((1, qb), jnp.int32), jnp.full((1, qb), 3e38, F32), 31, 16, kf)

    def low_bits_body(carry):
        it, u, cnt, _ = carry
        for _ in range(2):
            cu = u | lax.shift_left(jnp.int32(1), jnp.int32(15) - it)
            c = count(s_scr, _ordered_key_to_float(cu), one_f, zero_f)
            take = c >= kf
            u, cnt, it = jnp.where(take, cu, u), jnp.where(take, c, cnt), it + 1
        return it, u, cnt, unresolved(cnt)

    _, u, cnt_ge, _ = lax.while_loop(lambda cr: jnp.logical_and(cr[0] < 16, cr[3] > 0), low_bits_body,
                                     (jnp.int32(0), u, cnt, unresolved(cnt)))
    thr = jnp.where(short, -jnp.inf, _ordered_key_to_float(u))
    need = kf - count(s_scr, thr, one_f, zero_f, strict=True)
    has_tie = jnp.logical_and(jnp.logical_not(short), cnt_ge > kf)
    any_tie = jnp.max(jnp.where(has_tie, 1.0, 0.0)) > 0.5

    def attn_chunk(c, tie_seen, diag, ties):
        if ties:
            eq = jnp.where(s_scr[c] == thr, one_f, zero_f)
            p_scr[0] = eq.astype(p_scr.dtype)
            bias_scr[...] = _dot(tril_ref[...], p_scr[0])
        for rows in blocks:
            s = s_scr[c, rows, :]
            if ties:
                first = tie_seen + bias_scr[rows, :] < need
                sel = jnp.logical_or(s > thr, jnp.logical_and(s == thr, first))
            else:
                sel = s >= thr
            if diag is not None:
                sel = jnp.logical_and(sel, causal(rows, diag))
            bias_scr[rows, :] = jnp.where(sel, 0.0, NEG_INF)
        if ties:
            tie_seen = tie_seen + jnp.sum(eq, axis=0, keepdims=True)
        vt = vt_ref[c]
        top = lax.broadcasted_iota(jnp.int32, vt.shape, 0) < HEAD_DIM
        one_v = jnp.ones((), vt.dtype)
        vt_aug = (jnp.where(top, vt, one_v), jnp.where(top, one_v, vt))
        k_chunk = key_rows(kb_ref, c)
        for hh in range(N_HEADS):
            lg_scr[hh] = _dot_nt(k_chunk, qpad_scr[hh])
        alpha = []
        for hh in range(N_HEADS):
            top_x = None
            for rows in blocks:
                x = lg_scr[hh, rows, :] + bias_scr[rows, :]
                top_x = x if top_x is None else jnp.maximum(top_x, x)
            m_old = m_scr[hh]
            m_new = jnp.maximum(m_old, jnp.max(top_x, axis=0, keepdims=True))
            for rows in blocks:
                x = lg_scr[hh, rows, :] + bias_scr[rows, :]
                p_scr[hh, rows, :] = jnp.exp2(x - m_new).astype(p_scr.dtype)
            alpha.append(jnp.exp2(m_old - m_new))
            m_scr[hh] = m_new
        for hh in range(N_HEADS):
            acc_scr[hh] = acc_scr[hh] * alpha[hh] + _dot(vt_aug[hh // KV_GROUP], p_scr[hh])
        return tie_seen

    def run(ties):
        acc_scr[...] = jnp.zeros_like(acc_scr)
        m_scr[...] = jnp.full_like(m_scr, NEG_INF)
        seen = lax.fori_loop(0, n_full, lambda c, ts: attn_chunk(c, ts, None, ties), jnp.zeros((1, qb), F32))
        for d in range(n_diag):
            seen = attn_chunk(n_full + d, seen, d, ties)
        for j in range(KV_GROUP):
            a0 = acc_scr[j]
            a1 = acc_scr[KV_GROUP + j]
            o0 = a0[:HEAD_DIM] / a0[HEAD_DIM:HEAD_DIM + 1]
            o1 = a1[HEAD_DIM:] / a1[0:1]
            a_ref[:, tile(j)] = jnp.concatenate([o0, o1], axis=0).T.astype(a_ref.dtype)

    @pl.when(any_tie)
    def _():
        run(True)

    @pl.when(jnp.logical_not(any_tie))
    def _():
        run(False)


def _prompt_attn(q, iq, iwt, kb, vt, ikb, tril, batch, seq, qb, kc, k_sel):
    n = q.shape[0]
    nq = seq // qb
    nk = seq // kc
    rb = ELEMENTWISE_VREGS * SUBLANES * LANES // qb
    assert nk * (kc // rb) <= 256 and kc % rb == 0 and rb % 16 == 0 and qb % kc == 0
    qrow = lambda b, i: (b * nq + i, 0)
    keys = lambda b, i: (b, 0)
    once = dict(pipeline_mode=pl.Buffered(1))
    return pl.pallas_call(
        functools.partial(_prompt_attn_kernel, k_sel=k_sel),
        grid=(batch, nq),
        in_specs=[
            pl.BlockSpec((qb, 512), qrow),
            pl.BlockSpec((qb, 256), qrow),
            pl.BlockSpec((SUBLANES, qb), lambda b, i: (0, b * nq + i)),
            pl.BlockSpec((seq, LANES), keys, **once),
            pl.BlockSpec((nk, LANES, kc), lambda b, i: (b, 0, 0), **once),
            pl.BlockSpec((seq, LANES), keys, **once),
            pl.BlockSpec((kc, kc), lambda b, i: (0, 0), **once),
        ],
        out_specs=pl.BlockSpec((qb, 512), qrow),
        out_shape=jax.ShapeDtypeStruct((n, 512), MXU_DTYPE),
        scratch_shapes=[
            pltpu.VMEM((nk, kc, qb), F32),
            pltpu.VMEM((nk, kc, qb), jnp.bfloat16),
            pltpu.VMEM((N_HEADS, qb, LANES), MXU_DTYPE),
            pltpu.VMEM((N_IDX_HEADS, qb, LANES), MXU_DTYPE),
            pltpu.VMEM((kc, qb), F32),
            pltpu.VMEM((N_HEADS, kc, qb), F32),
            pltpu.VMEM((N_HEADS, kc, qb), MXU_DTYPE),
            pltpu.VMEM((N_HEADS, 1, qb), F32),
            pltpu.VMEM((N_HEADS, LANES, qb), F32),
        ],
        compiler_params=_params(2),
        name="prompt_attn",
    )(q, iq, iwt, kb, vt, ikb, tril)


def _sample_attn_kernel(pt_ref, q_ref, iq_ref, iw_ref, k_ref, v_ref, ik_ref, tri_ref, ck_ref, cv_ref, cik_ref,
                        a_ref, kt_buf, vt_buf, ikt_buf, sems, *, layer, rps, t_new, n_pages, page, k_sel):
    step = pl.program_id(0)
    n_steps = pl.num_programs(0)
    slot = step % 2
    past = n_pages * page
    n_tile = past // LANES

    def page_copies(st, sl):
        copies = []
        for r in range(rps):
            for p in range(n_pages):
                phys = pt_ref[st * rps + r, p]
                dst = pl.ds(p * page, page)
                copies.append(pltpu.make_async_copy(ck_ref.at[layer, phys], kt_buf.at[sl, r, :, :, dst], sems.at[sl, 0]))
                copies.append(pltpu.make_async_copy(cv_ref.at[layer, phys], vt_buf.at[sl, r, :, :, dst], sems.at[sl, 1]))
                copies.append(pltpu.make_async_copy(cik_ref.at[layer, phys], ikt_buf.at[sl, r, :, dst], sems.at[sl, 2]))
        return copies

    @pl.when(step == 0)
    def _():
        for cp in page_copies(0, 0):
            cp.start()

    @pl.when(step + 1 < n_steps)
    def _():
        for cp in page_copies(step + 1, 1 - slot):
            cp.start()

    for cp in page_copies(step, slot):
        cp.wait()

    n_row = rps * t_new
    trow = lax.broadcasted_iota(jnp.int32, (n_row, LANES), 0) % t_new
    tcol = lax.broadcasted_iota(jnp.int32, (n_row, LANES), 1)
    causal_new = tcol <= trow
    tri = tri_ref[...]
    qpos = past + lax.broadcasted_iota(jnp.int32, (n_row, 1), 0) % t_new
    short = qpos < k_sel

    def pad_keys(x):
        return jnp.concatenate([x, jnp.zeros((LANES - t_new, x.shape[1]), x.dtype)], axis=0).astype(MXU_DTYPE)

    s_past_l, s_new_l = [], []
    for r in range(rps):
        rows = slice(r * t_new, (r + 1) * t_new)
        iq = iq_ref[rows, :]
        iq4 = jnp.concatenate([iq[:, IDX_DIM * h:IDX_DIM * (h + 1)] for h in range(N_IDX_HEADS)], axis=0)
        iq4 = iq4.astype(MXU_DTYPE)
        iw = iw_ref[rows, :]
        wcol = [iw[:, h:h + 1] for h in range(N_IDX_HEADS)]
        zval = _zero_value(wcol)
        d_past = _dot(iq4, ikt_buf[slot, r].astype(MXU_DTYPE))
        d_new = _dot_nt(iq4, pad_keys(ik_ref[rows, :]))
        split = lambda d: [d[t_new * h:t_new * (h + 1)] for h in range(N_IDX_HEADS)]
        s_past_l.append(_indexer_scores(split(d_past), wcol, zval))
        s_new_l.append(_indexer_scores(split(d_new), wcol, zval))
    s_past = jnp.concatenate(s_past_l, axis=0)
    s_new = jnp.where(causal_new, jnp.concatenate(s_new_l, axis=0), NEG_INF)

    def count(pred):
        return (jnp.sum(jnp.where(pred(s_past), 1.0, 0.0), axis=1, keepdims=True)
                + jnp.sum(jnp.where(pred(s_new), 1.0, 0.0), axis=1, keepdims=True))

    thr, _ = _threshold_search(lambda t: count(lambda s: s >= t), n_row, float(k_sel))
    thr = jnp.where(short, -jnp.inf, thr)
    need = float(k_sel) - count(lambda s: s > thr)

    eq_past = jnp.where(s_past == thr, 1.0, 0.0)
    eq_tiles = [eq_past[:, LANES * c:LANES * (c + 1)] for c in range(n_tile)]
    before = _dot(jnp.concatenate(eq_tiles, axis=0).astype(tri.dtype), tri)
    seen = jnp.zeros((n_row, 1), F32)
    sel_tiles = []
    for c in range(n_tile):
        sc = s_past[:, LANES * c:LANES * (c + 1)]
        rank = seen + before[n_row * c:n_row * (c + 1)]
        sel_tiles.append(jnp.where(jnp.logical_or(sc > thr, jnp.logical_and(sc == thr, rank < need)), 0.0, NEG_INF))
        seen = seen + jnp.sum(eq_tiles[c], axis=1, keepdims=True)
    bias_past_all = jnp.concatenate(sel_tiles, axis=1)
    eq_new = jnp.where(s_new == thr, 1.0, 0.0)
    rank_new = seen + _dot(eq_new.astype(tri.dtype), tri)
    sel_new_all = jnp.logical_or(s_new > thr, jnp.logical_and(s_new == thr, rank_new < need))
    bias_new_all = jnp.where(jnp.logical_and(sel_new_all, causal_new), 0.0, NEG_INF)

    for r in range(rps):
        rows = slice(r * t_new, (r + 1) * t_new)
        bias_past = bias_past_all[rows]
        sel_new = bias_new_all[rows] == 0.0
        q = q_ref[rows, :]
        for g in range(N_KV_HEADS):
            lo = HEAD_DIM * g
            qg = jnp.concatenate([q[:, LANES * j + lo:LANES * j + lo + HEAD_DIM] for j in range(KV_GROUP)], axis=0)
            qg = qg.astype(MXU_DTYPE)
            lp = _dot(qg, kt_buf[slot, r, g].astype(MXU_DTYPE))
            ln = _dot_nt(qg, pad_keys(k_ref[rows, lo:lo + HEAD_DIM]))
            vt = vt_buf[slot, r, g].astype(MXU_DTYPE)
            vn = pad_keys(v_ref[rows, lo:lo + HEAD_DIM])
            for j in range(KV_GROUP):
                hs = slice(t_new * j, t_new * (j + 1))
                lpj = jnp.where(bias_past == 0.0, lp[hs], NEG_INF)
                lnj = jnp.where(sel_new, ln[hs], NEG_INF)
                mx = jnp.maximum(jnp.max(lpj, axis=1, keepdims=True), jnp.max(lnj, axis=1, keepdims=True))
                pp = jnp.exp2(lpj - mx)
                pn = jnp.exp2(lnj - mx)
                den = jnp.sum(pp, axis=1, keepdims=True) + jnp.sum(pn, axis=1, keepdims=True)
                o = _dot_nt(pp.astype(MXU_DTYPE), vt) + _dot(pn.astype(MXU_DTYPE), vn)
                a_ref[rows, LANES * j + lo:LANES * j + lo + HEAD_DIM] = (o / den).astype(a_ref.dtype)


def _sample_attn(page_table, q, iq, iw, k, v, ik, tri, ck_t, cv_t, cik_t, layer, rps, t_new, k_sel):
    n = q.shape[0]
    n_req, n_pages = page_table.shape
    page = ck_t.shape[-1]
    past = n_pages * page
    rows = rps * t_new
    blk = lambda w: pl.BlockSpec((rows, w), lambda s, pt: (s, 0))
    any_spec = pl.BlockSpec(memory_space=pl.ANY)
    return pl.pallas_call(
        functools.partial(_sample_attn_kernel, layer=layer, rps=rps, t_new=t_new, n_pages=n_pages, page=page,
                          k_sel=k_sel),
        grid_spec=pltpu.PrefetchScalarGridSpec(
            num_scalar_prefetch=1,
            grid=(n_req // rps,),
            in_specs=[blk(512), blk(256), blk(LANES), blk(LANES), blk(LANES), blk(IDX_DIM),
                      pl.BlockSpec((LANES, LANES), lambda s, pt: (0, 0)),
                      any_spec, any_spec, any_spec],
            out_specs=blk(512),
            scratch_shapes=[
                pltpu.VMEM((2, rps, N_KV_HEADS, HEAD_DIM, past), F32),
                pltpu.VMEM((2, rps, N_KV_HEADS, HEAD_DIM, past), F32),
                pltpu.VMEM((2, rps, IDX_DIM, past), F32),
                pltpu.SemaphoreType.DMA((2, 3)),
            ],
        ),
        out_shape=jax.ShapeDtypeStruct((n, 512), F32),
        compiler_params=_params(1),
        name="sample_attn",
    )(page_table, q, iq, iw, k, v, ik, tri, ck_t, cv_t, cik_t)


def _router_gates(x1, wr_ref, rb_ref):
    tm = x1.shape[0]
    x_hi = x1.astype(MXU_DTYPE)
    x_lo = (x1 - x_hi.astype(F32)).astype(MXU_DTYPE)
    logits = _dot(x_hi, wr_ref[0]) + (_dot(x_hi, wr_ref[1]) + _dot(x_lo, wr_ref[0]))
    aff = jax.nn.sigmoid(logits)
    biased = aff + rb_ref[...]
    lane = lax.broadcasted_iota(jnp.int32, (tm, LANES), 1)
    member = lane % EXPERTS_PER_GROUP
    group = lane // EXPERTS_PER_GROUP

    def other_member(x, d):
        return jnp.where(member + d < EXPERTS_PER_GROUP, pltpu.roll(x, LANES - d, 1),
                         pltpu.roll(x, EXPERTS_PER_GROUP - d, 1))

    def other_group(x, d):
        sh = EXPERTS_PER_GROUP * d
        return jnp.where(group + d < N_GROUPS, pltpu.roll(x, LANES - sh, 1), pltpu.roll(x, N_EXPERTS - sh, 1))

    def beaten_count(x, other, pos, n):
        cnt = jnp.zeros(x.shape, F32)
        for d in range(1, n):
            o = other(x, d)
            wins = jnp.logical_or(o > x, jnp.logical_and(o == x, pos + d >= n))
            cnt = cnt + jnp.where(wins, 1.0, 0.0)
        return cnt

    def group_sum(x):
        tot = x
        for d in range(1, EXPERTS_PER_GROUP):
            tot = tot + other_member(x, d)
        return tot

    top2 = beaten_count(biased, other_member, member, EXPERTS_PER_GROUP) < float(TOP_K_EXPERTS)
    grp_score = group_sum(jnp.where(top2, biased, 0.0))
    best = beaten_count(grp_score, other_group, group, N_GROUPS) < 1.0
    sel = jnp.logical_and(jnp.logical_and(top2, best), lane < N_EXPERTS)
    den = group_sum(jnp.where(sel, aff, 0.0))
    return jnp.where(sel, aff / jnp.where(sel, den, 1.0), 0.0)


def _mix_kernel(*refs, alpha, sample, t_new, blocks_per_seq):
    if sample:
        (x_ref, a_ref, wm_ref, cw_ref, cbias_ref, wb0_ref, wb1_ref, wo_ref, g_ref, b_ref, wr_ref, rb_ref,
         sp1_ref, sp2_ref, x1_ref, gates_ref, u_ref, ubuf) = refs
    else:
        (x_ref, a_ref, wm_ref, cw_ref, cbias_ref, wb0_ref, wb1_ref, wo_ref, g_ref, b_ref, wr_ref, rb_ref,
         x1_ref, gates_ref, u_ref, ubuf) = refs
    tm = x_ref.shape[0]
    hist = SUBLANES
    x = x_ref[...]
    xb = x.astype(MXU_DTYPE)
    c0, c1, c2, c3, c4 = 0, CONV_DIM, 2 * CONV_DIM, 3 * CONV_DIM, 3 * CONV_DIM + x.shape[1]
    ch = _dot(xb, wm_ref[:, c0:c1])
    cc = _dot(xb, wm_ref[:, c2:c3])
    u = cc * ch
    if sample:
        ubuf[0:hist, :] = jnp.zeros((hist, CONV_DIM), F32)
    else:
        @pl.when(pl.program_id(0) % blocks_per_seq == 0)
        def _():
            ubuf[0:hist, :] = jnp.zeros((hist, CONV_DIM), F32)
    ubuf[hist:hist + tm, :] = u
    prev1 = ubuf[hist - 1:hist - 1 + tm, :]
    prev2 = ubuf[hist - 2:hist - 2 + tm, :]
    if sample:
        t = lax.broadcasted_iota(jnp.int32, (tm, 1), 0) % t_new
        prev1 = jnp.where(t >= 1, prev1, sp1_ref[...])
        prev2 = jnp.where(t >= 2, prev2, sp2_ref[...])
        u_ref[...] = u
    else:
        u_ref[0] = u[tm - hist:tm, :]
        ubuf[0:hist, :] = u[tm - hist:tm, :]
    cw = cw_ref[...]
    cy = cbias_ref[...] + prev2 * cw[0:1, :]
    cy = cy + prev1 * cw[1:2, :]
    cy = cy + u * cw[2:3, :]
    n_part = 2 if tm % (2 * SUBLANES) == 0 else 1
    for part in range(n_part):
        rows = slice(tm // n_part * part, tm // n_part * (part + 1))
        xr, xbr = x[rows], xb[rows]
        c = _dot(xbr, wm_ref[:, c1:c2]) * cy[rows]
        a_d = _dot(a_ref[rows, :].astype(MXU_DTYPE), wb0_ref[...])
        c_d = _dot(c.astype(MXU_DTYPE), wb1_ref[...])
        merged = jax.nn.sigmoid(_dot(xbr, wm_ref[:, c3:c4])) * a_d
        merged = merged + jax.nn.sigmoid(_dot(xbr, wm_ref[:, c4:])) * c_d
        mix = _dot(merged.astype(MXU_DTYPE), wo_ref[...])
        x1 = _layer_norm(alpha * xr + mix, g_ref[...], b_ref[...])
        x1_ref[rows, :] = x1
        gates_ref[rows, :] = _router_gates(x1, wr_ref, rb_ref)


def _const_spec(shape, index):
    return pl.BlockSpec(shape, index, pipeline_mode=pl.Buffered(1))


def _mix(x, a, wm, conv_w, conv_b, wb0, wb1, wo, ln_g, ln_b, wr, rb, layer, tm, alpha, seq=None, sp1=None, sp2=None,
         t_new=None):
    n, d = x.shape
    sample = sp1 is not None
    row = lambda i: (i, 0)
    lay3 = lambda i: (layer, 0, 0)
    in_specs = [
        pl.BlockSpec((tm, d), row),
        pl.BlockSpec((tm, 512), row),
        _const_spec((None,) + wm.shape[1:], lay3),
        pl.BlockSpec((None, CONV_W, CONV_DIM), lay3),
        pl.BlockSpec((None, 1, CONV_DIM), lay3),
        _const_spec((None,) + wb0.shape[1:], lay3),
        _const_spec((None,) + wb1.shape[1:], lay3),
        _const_spec((None,) + wo.shape[1:], lay3),
        pl.BlockSpec((None, 1, d), lay3),
        pl.BlockSpec((None, 1, d), lay3),
        pl.BlockSpec(wr.shape, lambda i: (0, 0, 0)),
        pl.BlockSpec(rb.shape, lambda i: (0, 0)),
    ]
    args = [x, a, wm, conv_w, conv_b, wb0, wb1, wo, ln_g, ln_b, wr, rb]
    if sample:
        in_specs += [pl.BlockSpec((None, tm, CONV_DIM), lambda i: (layer, i, 0))] * 2
        args += [sp1, sp2]
        u_shape = jax.ShapeDtypeStruct((n, CONV_DIM), F32)
        u_spec = pl.BlockSpec((tm, CONV_DIM), row)
        blocks_per_seq = 1
    else:
        blocks_per_seq = seq // tm
        u_shape = jax.ShapeDtypeStruct((n // seq, SUBLANES, CONV_DIM), F32)
        u_spec = pl.BlockSpec((1, SUBLANES, CONV_DIM), lambda i: (i // blocks_per_seq, 0, 0))
    return pl.pallas_call(
        functools.partial(_mix_kernel, alpha=alpha, sample=sample, t_new=t_new, blocks_per_seq=blocks_per_seq),
        grid=(n // tm,),
        in_specs=in_specs,
        out_specs=[pl.BlockSpec((tm, d), row), pl.BlockSpec((tm, LANES), row), u_spec],
        out_shape=[jax.ShapeDtypeStruct((n, d), F32), jax.ShapeDtypeStruct((n, LANES), F32), u_shape],
        scratch_shapes=[pltpu.VMEM((SUBLANES + tm, CONV_DIM), F32)],
        compiler_params=_params(1),
        name="mix_sample" if sample else "mix_prompt",
    )(*args)


def _moe_kernel(x_ref, gates_ref, wg_ref, wu_ref, wd_ref, g_ref, b_ref, o_ref, acc_ref, *, alpha):
    x1 = x_ref[...]
    xb = x1.astype(MXU_DTYPE)
    gates = gates_ref[...]
    for e in range(N_EXPERTS):
        h = jax.nn.silu(_dot(xb, wg_ref[e])) * _dot(xb, wu_ref[e])
        h = h * gates[:, e:e + 1]
        y = _dot(h.astype(MXU_DTYPE), wd_ref[e])
        if e == 0:
            acc_ref[...] = y
        else:
            acc_ref[...] += y
    o_ref[...] = _layer_norm(alpha * x1 + acc_ref[...], g_ref[...], b_ref[...])


def _moe(x1, gates, wg, wu, wd, ln_g, ln_b, layer, tm, alpha):
    n, d = x1.shape
    row = lambda i: (i, 0)
    lay4 = lambda i: (layer, 0, 0, 0)
    lay3 = lambda i: (layer, 0, 0)
    return pl.pallas_call(
        functools.partial(_moe_kernel, alpha=alpha),
        grid=(n // tm,),
        in_specs=[
            pl.BlockSpec((tm, d), row),
            pl.BlockSpec((tm, LANES), row),
            _const_spec((None,) + wg.shape[1:], lay4),
            _const_spec((None,) + wu.shape[1:], lay4),
            _const_spec((None,) + wd.shape[1:], lay4),
            pl.BlockSpec((None, 1, d), lay3),
            pl.BlockSpec((None, 1, d), lay3),
        ],
        out_specs=pl.BlockSpec((tm, d), row),
        out_shape=jax.ShapeDtypeStruct((n, d), F32),
        scratch_shapes=[pltpu.VMEM((tm, d), F32)],
        compiler_params=_params(1),
        name="moe",
    )(x1, gates, wg, wu, wd, ln_g, ln_b)


def _rope_tables(pos):
    half = HEAD_DIM // 2
    inv = 1.0 / (ROPE_THETA ** (jnp.arange(half, dtype=F32) / half))
    ang = pos.astype(F32)[:, None] * inv[None, :]
    cos = jnp.cos(ang)
    sin = jnp.sin(ang)
    reps = LANES // HEAD_DIM
    return jnp.tile(cos, (1, 2 * reps)), jnp.tile(jnp.concatenate([-sin, sin], axis=1), (1, reps))


def _block(n, target):
    b = min(n, target)
    assert n % b == 0, (n, target)
    return b


def kernel(x_prompt, x_sample, cache_k, cache_v, cache_ik, state_conv, page_table, w_in, conv_w, conv_b, w_branch, w_out, ln1_g, ln1_b, ln2_g, ln2_b, w_router, router_bias, w_exp_gate, w_exp_up, w_exp_down):
    depth, d_model, _ = w_in.shape
    batch, seq, _ = x_prompt.shape
    n_req, t_new, _ = x_sample.shape
    n_pages = page_table.shape[1]
    page = cache_k.shape[2]
    past = n_pages * page
    assert t_new >= CONV_W - 1 and t_new % SUBLANES == 0 and page % LANES == 0
    alpha = float((2 * depth) ** 0.25)
    k_prompt_sel = min(TOPK_MAX, seq // 4)
    k_sample_sel = min(TOPK_MAX, (past + t_new) // 4)

    o = np.concatenate([[0], np.cumsum(SPLIT_SIZES)])
    perm = np.asarray(HEAD_PERM)
    wq = w_in[:, :, o[0]:o[1]].reshape(depth, d_model, N_HEADS, HEAD_DIM)[:, :, perm].reshape(depth, d_model, -1)
    w_ik = w_in[:, :, o[4]:o[5]]
    w_iw = jnp.pad(w_in[:, :, o[5]:o[6]], ((0, 0), (0, 0), (0, LANES - N_IDX_HEADS)))
    wa = jnp.concatenate([wq, w_in[:, :, o[1]:o[4]], w_ik, w_ik, w_iw], axis=-1).astype(MXU_DTYPE)
    wiw_t = jnp.transpose(w_iw[:, :, :SUBLANES], (0, 2, 1)).astype(MXU_DTYPE)
    wm = w_in[:, :, o[6]:].astype(MXU_DTYPE)
    wb0 = w_branch[:, 0].reshape(depth, N_HEADS, HEAD_DIM, d_model)[:, perm].reshape(depth, -1, d_model)
    wb0 = wb0.astype(MXU_DTYPE)
    wb1 = w_branch[:, 1].astype(MXU_DTYPE)
    wo = w_out.astype(MXU_DTYPE)
    wr32 = jnp.pad(w_router.astype(F32), ((0, 0), (0, LANES - N_EXPERTS)))
    wr_hi = wr32.astype(MXU_DTYPE)
    wr = jnp.stack([wr_hi, (wr32 - wr_hi.astype(F32)).astype(MXU_DTYPE)])
    rb = jnp.pad(router_bias.astype(F32), (0, LANES - N_EXPERTS)).reshape(1, LANES)
    wg = w_exp_gate.astype(MXU_DTYPE)
    wu = w_exp_up.astype(MXU_DTYPE)
    wd = w_exp_down.astype(MXU_DTYPE)
    ln1g, ln1b, ln2g, ln2b = (a.reshape(depth, 1, d_model) for a in (ln1_g, ln1_b, ln2_g, ln2_b))
    cbias = conv_b.reshape(depth, 1, CONV_DIM)

    cos_p, sin_p = _rope_tables(jnp.arange(seq, dtype=jnp.int32))
    pos_s = jnp.tile(past + jnp.arange(t_new, dtype=jnp.int32), n_req)
    cos_s, sin_s = _rope_tables(pos_s)
    ck_t = jnp.transpose(cache_k, (0, 1, 3, 4, 2))
    cv_t = jnp.transpose(cache_v, (0, 1, 3, 4, 2))
    cik_t = jnp.transpose(cache_ik, (0, 1, 3, 2))
    zeros_state = jnp.zeros((depth, n_req, t_new, CONV_DIM), F32)
    sp1 = zeros_state.at[:, :, 0].set(state_conv[:, :, 1]).reshape(depth, n_req * t_new, CONV_DIM)
    sp2 = zeros_state.at[:, :, :2].set(state_conv).reshape(depth, n_req * t_new, CONV_DIM)

    qb = _block(seq, PROMPT_QUERY_BLOCK)
    kc = _block(qb, PROMPT_KEY_CHUNK)
    tri = (np.arange(kc)[:, None] < np.arange(kc)[None, :])
    tril_p = jnp.asarray(tri.T, MXU_DTYPE)
    tri_s = jnp.asarray(tri[:LANES, :LANES], MXU_DTYPE)
    tm_p = _block(seq, 512)
    n_s = n_req * t_new
    tm_s = _block(n_s, 512)
    rps = next(r for r in (4, 2, 1) if n_req % r == 0)

    xp = x_prompt.reshape(batch * seq, d_model)
    xs = x_sample.reshape(n_s, d_model)
    outs = [[] for _ in range(8)]
    for l in range(depth):
        q, k, v, iq, ik, kb, vt, ikb, iwt = _attn_proj(xp, wa, l, cos_p, sin_p, tm_p, wiw_t=wiw_t, chunk=kc)
        a = _prompt_attn(q, iq, iwt, kb, vt, ikb, tril_p, batch, seq, qb, kc, k_prompt_sel)
        x1, gates, u_last = _mix(xp, a, wm, conv_w, cbias, wb0, wb1, wo, ln1g, ln1b, wr, rb, l, tm_p, alpha, seq=seq)
        xp = _moe(x1, gates, wg, wu, wd, ln2g, ln2b, l, tm_p, alpha)
        outs[0].append(k.reshape(batch, seq, N_KV_HEADS, HEAD_DIM))
        outs[1].append(v.reshape(batch, seq, N_KV_HEADS, HEAD_DIM))
        outs[2].append(ik.reshape(batch, seq, IDX_DIM))
        outs[3].append(u_last[:, SUBLANES - (CONV_W - 1):])

        q, k, v, iq, ik, iw = _attn_proj(xs, wa, l, cos_s, sin_s, tm_s)
        a = _sample_attn(page_table, q, iq, iw, k, v, ik, tri_s, ck_t, cv_t, cik_t, l, rps, t_new, k_sample_sel)
        x1, gates, u = _mix(xs, a, wm, conv_w, cbias, wb0, wb1, wo, ln1g, ln1b, wr, rb, l, tm_s, alpha,
                            sp1=sp1, sp2=sp2, t_new=t_new)
        xs = _moe(x1, gates, wg, wu, wd, ln2g, ln2b, l, tm_s, alpha)
        outs[4].append(k.reshape(n_req, t_new, N_KV_HEADS, HEAD_DIM))
        outs[5].append(v.reshape(n_req, t_new, N_KV_HEADS, HEAD_DIM))
        outs[6].append(ik.reshape(n_req, t_new, IDX_DIM))
        outs[7].append(u.reshape(n_req, t_new, CONV_DIM)[:, t_new - (CONV_W - 1):])
    stacked = [jnp.stack(o_) for o_ in outs]
    return (xp.reshape(batch, seq, d_model), xs.reshape(n_req, t_new, d_model), *stacked)
```

```python
import functools

import numpy as np
import jax
import jax.numpy as jnp
from jax import lax
from jax.experimental import pallas as pl
from jax.experimental.pallas import tpu as pltpu

N_HEADS = 8
HEAD_DIM = 64
N_KV_HEADS = 2
KV_GROUP = N_HEADS // N_KV_HEADS
N_IDX_HEADS = 4
IDX_DIM = 64
TOPK_MAX = 256
ROPE_THETA = 10000.0
CONV_DIM = 512
CONV_W = 3
N_EXPERTS = 16
N_GROUPS = 4
EXPERTS_PER_GROUP = N_EXPERTS // N_GROUPS
TOP_K_EXPERTS = 2
D_EXPERT = 256
LN_EPS = 1e-5
NEG_INF = -1e30
SPLIT_SIZES = (N_HEADS * HEAD_DIM, N_KV_HEADS * HEAD_DIM, N_KV_HEADS * HEAD_DIM, N_IDX_HEADS * IDX_DIM,
               IDX_DIM, N_IDX_HEADS, CONV_DIM, CONV_DIM, CONV_DIM)

LANES = 128
SUBLANES = 8
VMEM_LIMIT = 56 * 1024 * 1024
F32 = jnp.float32
MXU_DTYPE = jnp.bfloat16
F32_MIN_NORMAL = float(np.finfo(np.float32).tiny)
LOG2E = float(np.log2(np.e))
ELEMENTWISE_VREGS = 16
PROMPT_QUERY_BLOCK = 512
PROMPT_KEY_CHUNK = 256
HEAD_PERM = tuple(g * KV_GROUP + j for j in range(KV_GROUP) for g in range(N_KV_HEADS))

_NT = (((1,), (1,)), ((), ()))


def _dot(a, b):
    return jnp.dot(a, b, preferred_element_type=F32)


def _dot_nt(a, b):
    return lax.dot_general(a, b, _NT, preferred_element_type=F32)


def _params(n_grid):
    return pltpu.CompilerParams(dimension_semantics=("arbitrary",) * n_grid, vmem_limit_bytes=VMEM_LIMIT)


def _layer_norm(y, g, b):
    mu = jnp.mean(y, axis=-1, keepdims=True)
    yc = y - mu
    var = jnp.mean(yc * yc, axis=-1, keepdims=True)
    return yc * lax.rsqrt(var + LN_EPS) * g + b


def _attn_proj_kernel(*refs, prompt):
    if prompt:
        (x_ref, w_ref, cos_ref, sin_ref, wiw_ref,
         q_ref, k_ref, v_ref, iq_ref, ik_ref, kb_ref, vt_ref, ikb_ref, iwt_ref) = refs
    else:
        x_ref, w_ref, cos_ref, sin_ref, q_ref, k_ref, v_ref, iq_ref, ik_ref, iw_ref = refs
    tm = x_ref.shape[0]
    xb = x_ref[...].astype(MXU_DTYPE)
    cos = cos_ref[...]
    sin = sin_ref[...]
    lane = lax.broadcasted_iota(jnp.int32, (tm, LANES), 1)
    first_half = (lane & (HEAD_DIM // 2)) == 0

    def proj(col):
        return _dot(xb, w_ref[:, col:col + LANES])

    def rope(g):
        partner = jnp.where(first_half, pltpu.roll(g, LANES - HEAD_DIM // 2, 1), pltpu.roll(g, HEAD_DIM // 2, 1))
        return g * cos + partner * sin

    for j in range(4):
        q_ref[:, LANES * j:LANES * (j + 1)] = (rope(proj(LANES * j)) * (HEAD_DIM ** -0.5 * LOG2E)).astype(q_ref.dtype)
    k = rope(proj(512))
    k_ref[...] = k
    v = proj(640)
    v_ref[...] = v
    for j in range(2):
        iq_ref[:, LANES * j:LANES * (j + 1)] = rope(proj(768 + LANES * j)).astype(iq_ref.dtype)
    ik2 = rope(proj(1024))
    ik_ref[...] = ik2[:, :IDX_DIM]
    iw_scale = N_IDX_HEADS ** -0.5 * IDX_DIM ** -0.5
    if prompt:
        kb_ref[...] = k.astype(kb_ref.dtype)
        ikb_ref[...] = ik2.astype(ikb_ref.dtype)
        chunk = vt_ref.shape[2]
        for j in range(tm // chunk):
            vt_ref[j] = v[chunk * j:chunk * (j + 1), :].T.astype(vt_ref.dtype)
        iwt_ref[...] = _dot_nt(wiw_ref[...], xb) * iw_scale
    else:
        iw_ref[...] = proj(1152) * iw_scale


def _attn_proj(x, wa, layer, cos, sin, tm, wiw_t=None, chunk=None):
    n, d = x.shape
    prompt = wiw_t is not None
    nblk_pos = cos.shape[0] // tm
    row = lambda i: (i, 0)
    pos = lambda i: (i % nblk_pos, 0)
    low = MXU_DTYPE if prompt else F32
    outs = [((n, 512), low, (tm, 512), row),
            ((n, LANES), F32, (tm, LANES), row),
            ((n, LANES), F32, (tm, LANES), row),
            ((n, 256), low, (tm, 256), row),
            ((n, IDX_DIM), F32, (tm, IDX_DIM), row)]
    in_specs = [
        pl.BlockSpec((tm, d), row),
        pl.BlockSpec((None, d, wa.shape[2]), lambda i: (layer, 0, 0)),
        pl.BlockSpec((tm, LANES), pos),
        pl.BlockSpec((tm, LANES), pos),
    ]
    args = [x, wa, cos, sin]
    if prompt:
        in_specs.append(pl.BlockSpec((None,) + wiw_t.shape[1:], lambda i: (layer, 0, 0)))
        args.append(wiw_t)
        outs += [((n, LANES), MXU_DTYPE, (tm, LANES), row),
                 ((n // chunk, LANES, chunk), MXU_DTYPE, (tm // chunk, LANES, chunk), lambda i: (i, 0, 0)),
                 ((n, LANES), MXU_DTYPE, (tm, LANES), row),
                 ((SUBLANES, n), F32, (SUBLANES, tm), lambda i: (0, i))]
    else:
        outs.append(((n, LANES), F32, (tm, LANES), row))
    return pl.pallas_call(
        functools.partial(_attn_proj_kernel, prompt=prompt),
        grid=(n // tm,),
        in_specs=in_specs,
        out_specs=[pl.BlockSpec(blk, idx) for _, _, blk, idx in outs],
        out_shape=[jax.ShapeDtypeStruct(shape, dt) for shape, dt, _, _ in outs],
        compiler_params=_params(1),
        name="attn_proj_prompt" if prompt else "attn_proj_sample",
    )(*args)


def _ordered_key_to_float(u):
    int_min = jnp.int32(-2 ** 31)
    ck = u ^ int_min
    bits = jnp.where(ck >= 0, ck, ck ^ jnp.int32(2 ** 31 - 1))
    return lax.bitcast_convert_type(bits, F32)


def _bisect(count_ge, u, cnt_u, first_bit, n_bits, k_sel):
    def bit_body(it, carry):
        u, cnt_u = carry
        bit = lax.shift_left(jnp.int32(1), jnp.int32(first_bit) - it)
        cu = u | bit
        cnt = count_ge(_ordered_key_to_float(cu))
        take = cnt >= k_sel
        return jnp.where(take, cu, u), jnp.where(take, cnt, cnt_u)

    return lax.fori_loop(0, n_bits, bit_body, (u, cnt_u))


def _bisect_start(rows):
    return jnp.zeros((rows, 1), jnp.int32), jnp.full((rows, 1), 3e38, F32)


def _threshold_search(count_ge, rows, k_sel):
    u, cnt = _bisect(count_ge, *_bisect_start(rows), 31, 32, k_sel)
    return _ordered_key_to_float(u), cnt


def _floor_to_bf16(x):
    b = lax.bitcast_convert_type(x, jnp.int32)
    away = lax.shift_right_arithmetic(b, jnp.int32(31)) & jnp.int32(0xFFFF)
    return lax.bitcast_convert_type((b + away) & jnp.int32(-65536), F32).astype(jnp.bfloat16)


def _indexer_scores(dots, wcol, zval):
    s = None
    for d, w in zip(dots, wcol):
        t = jnp.maximum(d, 0.0) * w
        s = t if s is None else s + t
    return jnp.where(s == 0.0, zval, s)


def _zero_value(wcol):
    wmax = functools.reduce(jnp.maximum, wcol)
    return jnp.where(wmax < 0.0, -F32_MIN_NORMAL, 0.0).astype(F32)


def _prompt_attn_kernel(q_ref, iq_ref, iwt_ref, kb_ref, vt_ref, ikb_ref, tril_ref, a_ref,
                        s_scr, hi_scr, qpad_scr, iqp_scr, bias_scr, lg_scr, p_scr, m_scr, acc_scr,
                        *, k_sel):
    qb = q_ref.shape[0]
    kc = tril_ref.shape[0]
    n_diag = qb // kc
    rb = ELEMENTWISE_VREGS * SUBLANES * LANES // qb
    i = pl.program_id(1)
    n_full = i * n_diag
    kf = float(k_sel)
    lane = lax.broadcasted_iota(jnp.int32, (qb, LANES), 1)
    lo_half = lane < HEAD_DIM
    halves = (lo_half, jnp.logical_not(lo_half))
    tile = lambda t: slice(LANES * t, LANES * (t + 1))
    blocks = [slice(r, r + rb) for r in range(0, kc, rb)]

    def causal(rows, d):
        key = kc * d + rows.start + lax.broadcasted_iota(jnp.int32, (rb, qb), 0)
        return key <= lax.broadcasted_iota(jnp.int32, (rb, qb), 1)

    for h in range(N_IDX_HEADS):
        iqp_scr[h] = jnp.where(halves[h % 2], iq_ref[:, tile(h // 2)], jnp.zeros((), iq_ref.dtype))
    for hh in range(N_HEADS):
        g, j = divmod(hh, KV_GROUP)
        qpad_scr[hh] = jnp.where(halves[g], q_ref[:, tile(j)], jnp.zeros((), q_ref.dtype))
    iwt = iwt_ref[...]
    wrow = [iwt[h:h + 1, :] for h in range(N_IDX_HEADS)]
    zval = _zero_value(wrow)

    def key_rows(ref, c):
        return ref[pl.ds(pl.multiple_of(c * kc, kc), kc), :]

    assert N_IDX_HEADS * n_diag <= N_HEADS

    def score_chunks(c0, diag):
        for j in range(n_diag):
            ikc = key_rows(ikb_ref, c0 + j)
            for h in range(N_IDX_HEADS):
                lg_scr[N_IDX_HEADS * j + h] = _dot_nt(ikc, iqp_scr[h])
        for j in range(n_diag):
            for rows in blocks:
                s = None
                for h in range(N_IDX_HEADS):
                    term = jnp.maximum(lg_scr[N_IDX_HEADS * j + h, rows, :], 0.0) * wrow[h]
                    s = term if s is None else s + term
                s = jnp.where(s == 0.0, zval, s)
                if diag:
                    s = jnp.where(causal(rows, j), s, NEG_INF)
                s_scr[c0 + j, rows, :] = s
                hi_scr[c0 + j, rows, :] = _floor_to_bf16(s)

    def score_body(g, carry):
        score_chunks(g * n_diag, False)
        return carry

    lax.fori_loop(0, i, score_body, 0)
    score_chunks(n_full, True)

    def count(scr, thr_row, one, zero, strict=False):
        thr_b = jnp.broadcast_to(thr_row, (rb, qb))

        def body(c, acc):
            for rows in blocks:
                x = scr[c, rows, :]
                acc = acc + jnp.where(x > thr_b if strict else x >= thr_b, one, zero)
            return acc

        acc = lax.fori_loop(0, n_full + n_diag, body, jnp.zeros((rb, qb), one.dtype))
        return jnp.sum(acc.astype(F32), axis=0, keepdims=True)

    one_h, zero_h = jnp.ones((), hi_scr.dtype), jnp.zeros((), hi_scr.dtype)
    one_f, zero_f = jnp.ones((), F32), jnp.zeros((), F32)
    qpos = i * qb + lax.broadcasted_iota(jnp.int32, (1, qb), 1)
    short = qpos < k_sel

    def unresolved(cnt):
        return jnp.max(jnp.where(jnp.logical_or(short, cnt == kf), 0, 1)).astype(jnp.int32)

    u, cnt = _bisect(lambda t: count(hi_scr, t.astype(hi_scr.dtype), one_h, zero_h),
                     jnp.zeros((1, qb), jnp.int32), jnp.full((1, qb), 3e38, F32), 31, 16, kf)

    def low_bits_body(carry):
        it, u, cnt, _ = carry
        for _ in range(2):
            cu = u | lax.shift_left(jnp.int32(1), jnp.int32(15) - it)
            c = count(s_scr, _ordered_key_to_float(cu), one_f, zero_f)
            take = c >= kf
            u, cnt, it = jnp.where(take, cu, u), jnp.where(take, c, cnt), it + 1
        return it, u, cnt, unresolved(cnt)

    _, u, cnt_ge, _ = lax.while_loop(lambda cr: jnp.logical_and(cr[0] < 16, cr[3] > 0), low_bits_body,
                                     (jnp.int32(0), u, cnt, unresolved(cnt)))
    thr = jnp.where(short, -jnp.inf, _ordered_key_to_float(u))
    need = kf - count(s_scr, thr, one_f, zero_f, strict=True)
    has_tie = jnp.logical_and(jnp.logical_not(short), cnt_ge > kf)
    any_tie = jnp.max(jnp.where(has_tie, 1.0, 0.0)) > 0.5

    def attn_chunk(c, tie_seen, diag, ties):
        if ties:
            eq = jnp.where(s_scr[c] == thr, one_f, zero_f)
            p_scr[0] = eq.astype(p_scr.dtype)
            bias_scr[...] = _dot(tril_ref[...], p_scr[0])
        for rows in blocks:
            s = s_scr[c, rows, :]
            if ties:
                first = tie_seen + bias_scr[rows, :] < need
                sel = jnp.logical_or(s > thr, jnp.logical_and(s == thr, first))
            else:
                sel = s >= thr
            if diag is not None:
                sel = jnp.logical_and(sel, causal(rows, diag))
            bias_scr[rows, :] = jnp.where(sel, 0.0, NEG_INF)
        if ties:
            tie_seen = tie_seen + jnp.sum(eq, axis=0, keepdims=True)
        vt = vt_ref[c]
        top = lax.broadcasted_iota(jnp.int32, vt.shape, 0) < HEAD_DIM
        one_v = jnp.ones((), vt.dtype)
        vt_aug = (jnp.where(top, vt, one_v), jnp.where(top, one_v, vt))
        k_chunk = key_rows(kb_ref, c)
        for hh in range(N_HEADS):
            lg_scr[hh] = _dot_nt(k_chunk, qpad_scr[hh])
        alpha = []
        for hh in range(N_HEADS):
            top_x = None
            for rows in blocks:
                x = lg_scr[hh, rows, :] + bias_scr[rows, :]
                top_x = x if top_x is None else jnp.maximum(top_x, x)
            m_old = m_scr[hh]
            m_new = jnp.maximum(m_old, jnp.max(top_x, axis=0, keepdims=True))
            for rows in blocks:
                x = lg_scr[hh, rows, :] + bias_scr[rows, :]
                p_scr[hh, rows, :] = jnp.exp2(x - m_new).astype(p_scr.dtype)
            alpha.append(jnp.exp2(m_old - m_new))
            m_scr[hh] = m_new
        for hh in range(N_HEADS):
            acc_scr[hh] = acc_scr[hh] * alpha[hh] + _dot(vt_aug[hh // KV_GROUP], p_scr[hh])
        return tie_seen

    def run(ties):
        acc_scr[...] = jnp.zeros_like(acc_scr)
        m_scr[...] = jnp.full_like(m_scr, NEG_INF)
        seen = lax.fori_loop(0, n_full, lambda c, ts: attn_chunk(c, ts, None, ties), jnp.zeros((1, qb), F32))
        for d in range(n_diag):
            seen = attn_chunk(n_full + d, seen, d, ties)
        for j in range(KV_GROUP):
            a0 = acc_scr[j]
            a1 = acc_scr[KV_GROUP + j]
            o0 = a0[:HEAD_DIM] / a0[HEAD_DIM:HEAD_DIM + 1]
            o1 = a1[HEAD_DIM:] / a1[0:1]
            a_ref[:, tile(j)] = jnp.concatenate([o0, o1], axis=0).T.astype(a_ref.dtype)

    @pl.when(any_tie)
    def _():
        run(True)

    @pl.when(jnp.logical_not(any_tie))
    def _():
        run(False)


def _prompt_attn(q, iq, iwt, kb, vt, ikb, tril, batch, seq, qb, kc, k_sel):
    n = q.shape[0]
    nq = seq // qb
    nk = seq // kc
    rb = ELEMENTWISE_VREGS * SUBLANES * LANES // qb
    assert nk * (kc // rb) <= 256 and kc % rb == 0 and rb % 16 == 0 and qb % kc == 0
    qrow = lambda b, i: (b * nq + i, 0)
    keys = lambda b, i: (b, 0)
    once = dict(pipeline_mode=pl.Buffered(1))
    return pl.pallas_call(
        functools.partial(_prompt_attn_kernel, k_sel=k_sel),
        grid=(batch, nq),
        in_specs=[
            pl.BlockSpec((qb, 512), qrow),
            pl.BlockSpec((qb, 256), qrow),
            pl.BlockSpec((SUBLANES, qb), lambda b, i: (0, b * nq + i)),
            pl.BlockSpec((seq, LANES), keys, **once),
            pl.BlockSpec((nk, LANES, kc), lambda b, i: (b, 0, 0), **once),
            pl.BlockSpec((seq, LANES), keys, **once),
            pl.BlockSpec((kc, kc), lambda b, i: (0, 0), **once),
        ],
        out_specs=pl.BlockSpec((qb, 512), qrow),
        out_shape=jax.ShapeDtypeStruct((n, 512), MXU_DTYPE),
        scratch_shapes=[
            pltpu.VMEM((nk, kc, qb), F32),
            pltpu.VMEM((nk, kc, qb), jnp.bfloat16),
            pltpu.VMEM((N_HEADS, qb, LANES), MXU_DTYPE),
            pltpu.VMEM((N_IDX_HEADS, qb, LANES), MXU_DTYPE),
            pltpu.VMEM((kc, qb), F32),
            pltpu.VMEM((N_HEADS, kc, qb), F32),
            pltpu.VMEM((N_HEADS, kc, qb), MXU_DTYPE),
            pltpu.VMEM((N_HEADS, 1, qb), F32),
            pltpu.VMEM((N_HEADS, LANES, qb), F32),
        ],
        compiler_params=_params(2),
        name="prompt_attn",
    )(q, iq, iwt, kb, vt, ikb, tril)


def _sample_attn_kernel(pt_ref, q_ref, iq_ref, iw_ref, k_ref, v_ref, ik_ref, tri_ref, ck_ref, cv_ref, cik_ref,
                        a_ref, kt_buf, vt_buf, ikt_buf, sems, *, layer, rps, t_new, n_pages, page, k_sel):
    step = pl.program_id(0)
    n_steps = pl.num_programs(0)
    slot = step % 2
    past = n_pages * page
    n_tile = past // LANES

    def page_copies(st, sl):
        copies = []
        for r in range(rps):
            for p in range(n_pages):
                phys = pt_ref[st * rps + r, p]
                dst = pl.ds(p * page, page)
                copies.append(pltpu.make_async_copy(ck_ref.at[layer, phys], kt_buf.at[sl, r, :, :, dst], sems.at[sl, 0]))
                copies.append(pltpu.make_async_copy(cv_ref.at[layer, phys], vt_buf.at[sl, r, :, :, dst], sems.at[sl, 1]))
                copies.append(pltpu.make_async_copy(cik_ref.at[layer, phys], ikt_buf.at[sl, r, :, dst], sems.at[sl, 2]))
        return copies

    @pl.when(step == 0)
    def _():
        for cp in page_copies(0, 0):
            cp.start()

    @pl.when(step + 1 < n_steps)
    def _():
        for cp in page_copies(step + 1, 1 - slot):
            cp.start()

    for cp in page_copies(step, slot):
        cp.wait()

    n_row = rps * t_new
    trow = lax.broadcasted_iota(jnp.int32, (n_row, LANES), 0) % t_new
    tcol = lax.broadcasted_iota(jnp.int32, (n_row, LANES), 1)
    causal_new = tcol <= trow
    tri = tri_ref[...]
    qpos = past + lax.broadcasted_iota(jnp.int32, (n_row, 1), 0) % t_new
    short = qpos < k_sel

    def pad_keys(x):
        return jnp.concatenate([x, jnp.zeros((LANES - t_new, x.shape[1]), x.dtype)], axis=0).astype(MXU_DTYPE)

    s_past_l, s_new_l = [], []
    for r in range(rps):
        rows = slice(r * t_new, (r + 1) * t_new)
        iq = iq_ref[rows, :]
        iq4 = jnp.concatenate([iq[:, IDX_DIM * h:IDX_DIM * (h + 1)] for h in range(N_IDX_HEADS)], axis=0)
        iq4 = iq4.astype(MXU_DTYPE)
        iw = iw_ref[rows, :]
        wcol = [iw[:, h:h + 1] for h in range(N_IDX_HEADS)]
        zval = _zero_value(wcol)
        d_past = _dot(iq4, ikt_buf[slot, r].astype(MXU_DTYPE))
        d_new = _dot_nt(iq4, pad_keys(ik_ref[rows, :]))
        split = lambda d: [d[t_new * h:t_new * (h + 1)] for h in range(N_IDX_HEADS)]
        s_past_l.append(_indexer_scores(split(d_past), wcol, zval))
        s_new_l.append(_indexer_scores(split(d_new), wcol, zval))
    s_past = jnp.concatenate(s_past_l, axis=0)
    s_new = jnp.where(causal_new, jnp.concatenate(s_new_l, axis=0), NEG_INF)

    def count(pred):
        return (jnp.sum(jnp.where(pred(s_past), 1.0, 0.0), axis=1, keepdims=True)
                + jnp.sum(jnp.where(pred(s_new), 1.0, 0.0), axis=1, keepdims=True))

    thr, _ = _threshold_search(lambda t: count(lambda s: s >= t), n_row, float(k_sel))
    thr = jnp.where(short, -jnp.inf, thr)
    need = float(k_sel) - count(lambda s: s > thr)

    eq_past = jnp.where(s_past == thr, 1.0, 0.0)
    eq_tiles = [eq_past[:, LANES * c:LANES * (c + 1)] for c in range(n_tile)]
    before = _dot(jnp.concatenate(eq_tiles, axis=0).astype(tri.dtype), tri)
    seen = jnp.zeros((n_row, 1), F32)
    sel_tiles = []
    for c in range(n_tile):
        sc = s_past[:, LANES * c:LANES * (c + 1)]
        rank = seen + before[n_row * c:n_row * (c + 1)]
        sel_tiles.append(jnp.where(jnp.logical_or(sc > thr, jnp.logical_and(sc == thr, rank < need)), 0.0, NEG_INF))
        seen = seen + jnp.sum(eq_tiles[c], axis=1, keepdims=True)
    bias_past_all = jnp.concatenate(sel_tiles, axis=1)
    eq_new = jnp.where(s_new == thr, 1.0, 0.0)
    rank_new = seen + _dot(eq_new.astype(tri.dtype), tri)
    sel_new_all = jnp.logical_or(s_new > thr, jnp.logical_and(s_new == thr, rank_new < need))
    bias_new_all = jnp.where(jnp.logical_and(sel_new_all, causal_new), 0.0, NEG_INF)

    for r in range(rps):
        rows = slice(r * t_new, (r + 1) * t_new)
        bias_past = bias_past_all[rows]
        sel_new = bias_new_all[rows] == 0.0
        q = q_ref[rows, :]
        for g in range(N_KV_HEADS):
            lo = HEAD_DIM * g
            qg = jnp.concatenate([q[:, LANES * j + lo:LANES * j + lo + HEAD_DIM] for j in range(KV_GROUP)], axis=0)
            qg = qg.astype(MXU_DTYPE)
            lp = _dot(qg, kt_buf[slot, r, g].astype(MXU_DTYPE))
            ln = _dot_nt(qg, pad_keys(k_ref[rows, lo:lo + HEAD_DIM]))
            vt = vt_buf[slot, r, g].astype(MXU_DTYPE)
            vn = pad_keys(v_ref[rows, lo:lo + HEAD_DIM])
            for j in range(KV_GROUP):
                hs = slice(t_new * j, t_new * (j + 1))
                lpj = jnp.where(bias_past == 0.0, lp[hs], NEG_INF)
                lnj = jnp.where(sel_new, ln[hs], NEG_INF)
                mx = jnp.maximum(jnp.max(lpj, axis=1, keepdims=True), jnp.max(lnj, axis=1, keepdims=True))
                pp = jnp.exp2(lpj - mx)
                pn = jnp.exp2(lnj - mx)
                den = jnp.sum(pp, axis=1, keepdims=True) + jnp.sum(pn, axis=1, keepdims=True)
                o = _dot_nt(pp.astype(MXU_DTYPE), vt) + _dot(pn.astype(MXU_DTYPE), vn)
                a_ref[rows, LANES * j + lo:LANES * j + lo + HEAD_DIM] = (o / den).astype(a_ref.dtype)


def _sample_attn(page_table, q, iq, iw, k, v, ik, tri, ck_t, cv_t, cik_t, layer, rps, t_new, k_sel):
    n = q.shape[0]
    n_req, n_pages = page_table.shape
    page = ck_t.shape[-1]
    past = n_pages * page
    rows = rps * t_new
    blk = lambda w: pl.BlockSpec((rows, w), lambda s, pt: (s, 0))
    any_spec = pl.BlockSpec(memory_space=pl.ANY)
    return pl.pallas_call(
        functools.partial(_sample_attn_kernel, layer=layer, rps=rps, t_new=t_new, n_pages=n_pages, page=page,
                          k_sel=k_sel),
        grid_spec=pltpu.PrefetchScalarGridSpec(
            num_scalar_prefetch=1,
            grid=(n_req // rps,),
            in_specs=[blk(512), blk(256), blk(LANES), blk(LANES), blk(LANES), blk(IDX_DIM),
                      pl.BlockSpec((LANES, LANES), lambda s, pt: (0, 0)),
                      any_spec, any_spec, any_spec],
            out_specs=blk(512),
            scratch_shapes=[
                pltpu.VMEM((2, rps, N_KV_HEADS, HEAD_DIM, past), F32),
                pltpu.VMEM((2, rps, N_KV_HEADS, HEAD_DIM, past), F32),
                pltpu.VMEM((2, rps, IDX_DIM, past), F32),
                pltpu.SemaphoreType.DMA((2, 3)),
            ],
        ),
        out_shape=jax.ShapeDtypeStruct((n, 512), F32),
        compiler_params=_params(1),
        name="sample_attn",
    )(page_table, q, iq, iw, k, v, ik, tri, ck_t, cv_t, cik_t)


def _router_gates(x1, wr_ref, rb_ref):
    tm = x1.shape[0]
    x_hi = x1.astype(MXU_DTYPE)
    x_lo = (x1 - x_hi.astype(F32)).astype(MXU_DTYPE)
    logits = _dot(x_hi, wr_ref[0]) + (_dot(x_hi, wr_ref[1]) + _dot(x_lo, wr_ref[0]))
    aff = jax.nn.sigmoid(logits)
    biased = aff + rb_ref[...]
    lane = lax.broadcasted_iota(jnp.int32, (tm, LANES), 1)
    member = lane % EXPERTS_PER_GROUP
    group = lane // EXPERTS_PER_GROUP

    def other_member(x, d):
        return jnp.where(member + d < EXPERTS_PER_GROUP, pltpu.roll(x, LANES - d, 1),
                         pltpu.roll(x, EXPERTS_PER_GROUP - d, 1))

    def other_group(x, d):
        sh = EXPERTS_PER_GROUP * d
        return jnp.where(group + d < N_GROUPS, pltpu.roll(x, LANES - sh, 1), pltpu.roll(x, N_EXPERTS - sh, 1))

    def beaten_count(x, other, pos, n):
        cnt = jnp.zeros(x.shape, F32)
        for d in range(1, n):
            o = other(x, d)
            wins = jnp.logical_or(o > x, jnp.logical_and(o == x, pos + d >= n))
            cnt = cnt + jnp.where(wins, 1.0, 0.0)
        return cnt

    def group_sum(x):
        tot = x
        for d in range(1, EXPERTS_PER_GROUP):
            tot = tot + other_member(x, d)
        return tot

    top2 = beaten_count(biased, other_member, member, EXPERTS_PER_GROUP) < float(TOP_K_EXPERTS)
    grp_score = group_sum(jnp.where(top2, biased, 0.0))
    best = beaten_count(grp_score, other_group, group, N_GROUPS) < 1.0
    sel = jnp.logical_and(jnp.logical_and(top2, best), lane < N_EXPERTS)
    den = group_sum(jnp.where(sel, aff, 0.0))
    return jnp.where(sel, aff / jnp.where(sel, den, 1.0), 0.0)


def _mix_kernel(*refs, alpha, sample, t_new, blocks_per_seq):
    if sample:
        (x_ref, a_ref, wm_ref, cw_ref, cbias_ref, wb0_ref, wb1_ref, wo_ref, g_ref, b_ref, wr_ref, rb_ref,
         sp1_ref, sp2_ref, x1_ref, gates_ref, u_ref, ubuf) = refs
    else:
        (x_ref, a_ref, wm_ref, cw_ref, cbias_ref, wb0_ref, wb1_ref, wo_ref, g_ref, b_ref, wr_ref, rb_ref,
         x1_ref, gates_ref, u_ref, ubuf) = refs
    tm = x_ref.shape[0]
    hist = SUBLANES
    x = x_ref[...]
    xb = x.astype(MXU_DTYPE)
    c0, c1, c2, c3, c4 = 0, CONV_DIM, 2 * CONV_DIM, 3 * CONV_DIM, 3 * CONV_DIM + x.shape[1]
    ch = _dot(xb, wm_ref[:, c0:c1])
    cc = _dot(xb, wm_ref[:, c2:c3])
    u = cc * ch
    if sample:
        ubuf[0:hist, :] = jnp.zeros((hist, CONV_DIM), F32)
    else:
        @pl.when(pl.program_id(0) % blocks_per_seq == 0)
        def _():
            ubuf[0:hist, :] = jnp.zeros((hist, CONV_DIM), F32)
    ubuf[hist:hist + tm, :] = u
    prev1 = ubuf[hist - 1:hist - 1 + tm, :]
    prev2 = ubuf[hist - 2:hist - 2 + tm, :]
    if sample:
        t = lax.broadcasted_iota(jnp.int32, (tm, 1), 0) % t_new
        prev1 = jnp.where(t >= 1, prev1, sp1_ref[...])
        prev2 = jnp.where(t >= 2, prev2, sp2_ref[...])
        u_ref[...] = u
    else:
        u_ref[0] = u[tm - hist:tm, :]
        ubuf[0:hist, :] = u[tm - hist:tm, :]
    cw = cw_ref[...]
    cy = cbias_ref[...] + prev2 * cw[0:1, :]
    cy = cy + prev1 * cw[1:2, :]
    cy = cy + u * cw[2:3, :]
    n_part = 2 if tm % (2 * SUBLANES) == 0 else 1
    for part in range(n_part):
        rows = slice(tm // n_part * part, tm // n_part * (part + 1))
        xr, xbr = x[rows], xb[rows]
        c = _dot(xbr, wm_ref[:, c1:c2]) * cy[rows]
        a_d = _dot(a_ref[rows, :].astype(MXU_DTYPE), wb0_ref[...])
        c_d = _dot(c.astype(MXU_DTYPE), wb1_ref[...])
        merged = jax.nn.sigmoid(_dot(xbr, wm_ref[:, c3:c4])) * a_d
        merged = merged + jax.nn.sigmoid(_dot(xbr, wm_ref[:, c4:])) * c_d
        mix = _dot(merged.astype(MXU_DTYPE), wo_ref[...])
        x1 = _layer_norm(alpha * xr + mix, g_ref[...], b_ref[...])
        x1_ref[rows, :] = x1
        gates_ref[rows, :] = _router_gates(x1, wr_ref, rb_ref)


def _const_spec(shape, index):
    return pl.BlockSpec(shape, index, pipeline_mode=pl.Buffered(1))


def _mix(x, a, wm, conv_w, conv_b, wb0, wb1, wo, ln_g, ln_b, wr, rb, layer, tm, alpha, seq=None, sp1=None, sp2=None,
         t_new=None):
    n, d = x.shape
    sample = sp1 is not None
    row = lambda i: (i, 0)
    lay3 = lambda i: (layer, 0, 0)
    in_specs = [
        pl.BlockSpec((tm, d), row),
        pl.BlockSpec((tm, 512), row),
        _const_spec((None,) + wm.shape[1:], lay3),
        pl.BlockSpec((None, CONV_W, CONV_DIM), lay3),
        pl.BlockSpec((None, 1, CONV_DIM), lay3),
        _const_spec((None,) + wb0.shape[1:], lay3),
        _const_spec((None,) + wb1.shape[1:], lay3),
        _const_spec((None,) + wo.shape[1:], lay3),
        pl.BlockSpec((None, 1, d), lay3),
        pl.BlockSpec((None, 1, d), lay3),
        pl.BlockSpec(wr.shape, lambda i: (0, 0, 0)),
        pl.BlockSpec(rb.shape, lambda i: (0, 0)),
    ]
    args = [x, a, wm, conv_w, conv_b, wb0, wb1, wo, ln_g, ln_b, wr, rb]
    if sample:
        in_specs += [pl.BlockSpec((None, tm, CONV_DIM), lambda i: (layer, i, 0))] * 2
        args += [sp1, sp2]
        u_shape = jax.ShapeDtypeStruct((n, CONV_DIM), F32)
        u_spec = pl.BlockSpec((tm, CONV_DIM), row)
        blocks_per_seq = 1
    else:
        blocks_per_seq = seq // tm
        u_shape = jax.ShapeDtypeStruct((n // seq, SUBLANES, CONV_DIM), F32)
        u_spec = pl.BlockSpec((1, SUBLANES, CONV_DIM), lambda i: (i // blocks_per_seq, 0, 0))
    return pl.pallas_call(
        functools.partial(_mix_kernel, alpha=alpha, sample=sample, t_new=t_new, blocks_per_seq=blocks_per_seq),
        grid=(n // tm,),
        in_specs=in_specs,
        out_specs=[pl.BlockSpec((tm, d), row), pl.BlockSpec((tm, LANES), row), u_spec],
        out_shape=[jax.ShapeDtypeStruct((n, d), F32), jax.ShapeDtypeStruct((n, LANES), F32), u_shape],
        scratch_shapes=[pltpu.VMEM((SUBLANES + tm, CONV_DIM), F32)],
        compiler_params=_params(1),
        name="mix_sample" if sample else "mix_prompt",
    )(*args)


def _moe_kernel(x_ref, gates_ref, wg_ref, wu_ref, wd_ref, g_ref, b_ref, o_ref, acc_ref, *, alpha):
    x1 = x_ref[...]
    xb = x1.astype(MXU_DTYPE)
    gates = gates_ref[...]
    for e in range(N_EXPERTS):
        h = jax.nn.silu(_dot(xb, wg_ref[e])) * _dot(xb, wu_ref[e])
        h = h * gates[:, e:e + 1]
        y = _dot(h.astype(MXU_DTYPE), wd_ref[e])
        if e == 0:
            acc_ref[...] = y
        else:
            acc_ref[...] += y
    o_ref[...] = _layer_norm(alpha * x1 + acc_ref[...], g_ref[...], b_ref[...])


def _moe(x1, gates, wg, wu, wd, ln_g, ln_b, layer, tm, alpha):
    n, d = x1.shape
    row = lambda i: (i, 0)
    lay4 = lambda i: (layer, 0, 0, 0)
    lay3 = lambda i: (layer, 0, 0)
    return pl.pallas_call(
        functools.partial(_moe_kernel, alpha=alpha),
        grid=(n // tm,),
        in_specs=[
            pl.BlockSpec((tm, d), row),
            pl.BlockSpec((tm, LANES), row),
            _const_spec((None,) + wg.shape[1:], lay4),
            _const_spec((None,) + wu.shape[1:], lay4),
            _const_spec((None,) + wd.shape[1:], lay4),
            pl.BlockSpec((None, 1, d), lay3),
            pl.BlockSpec((None, 1, d), lay3),
        ],
        out_specs=pl.BlockSpec((tm, d), row),
        out_shape=jax.ShapeDtypeStruct((n, d), F32),
        scratch_shapes=[pltpu.VMEM((tm, d), F32)],
        compiler_params=_params(1),
        name="moe",
    )(x1, gates, wg, wu, wd, ln_g, ln_b)


def _rope_tables(pos):
    half = HEAD_DIM // 2
    inv = 1.0 / (ROPE_THETA ** (jnp.arange(half, dtype=F32) / half))
    ang = pos.astype(F32)[:, None] * inv[None, :]
    cos = jnp.cos(ang)
    sin = jnp.sin(ang)
    reps = LANES // HEAD_DIM
    return jnp.tile(cos, (1, 2 * reps)), jnp.tile(jnp.concatenate([-sin, sin], axis=1), (1, reps))


def _block(n, target):
    b = min(n, target)
    assert n % b == 0, (n, target)
    return b


def kernel(x_prompt, x_sample, cache_k, cache_v, cache_ik, state_conv, page_table, w_in, conv_w, conv_b, w_branch, w_out, ln1_g, ln1_b, ln2_g, ln2_b, w_router, router_bias, w_exp_gate, w_exp_up, w_exp_down):
    depth, d_model, _ = w_in.shape
    batch, seq, _ = x_prompt.shape
    n_req, t_new, _ = x_sample.shape
    n_pages = page_table.shape[1]
    page = cache_k.shape[2]
    past = n_pages * page
    assert t_new >= CONV_W - 1 and t_new % SUBLANES == 0 and page % LANES == 0
    alpha = float((2 * depth) ** 0.25)
    k_prompt_sel = min(TOPK_MAX, seq // 4)
    k_sample_sel = min(TOPK_MAX, (past + t_new) // 4)

    o = np.concatenate([[0], np.cumsum(SPLIT_SIZES)])
    perm = np.asarray(HEAD_PERM)
    wq = w_in[:, :, o[0]:o[1]].reshape(depth, d_model, N_HEADS, HEAD_DIM)[:, :, perm].reshape(depth, d_model, -1)
    w_ik = w_in[:, :, o[4]:o[5]]
    w_iw = jnp.pad(w_in[:, :, o[5]:o[6]], ((0, 0), (0, 0), (0, LANES - N_IDX_HEADS)))
    wa = jnp.concatenate([wq, w_in[:, :, o[1]:o[4]], w_ik, w_ik, w_iw], axis=-1).astype(MXU_DTYPE)
    wiw_t = jnp.transpose(w_iw[:, :, :SUBLANES], (0, 2, 1)).astype(MXU_DTYPE)
    wm = w_in[:, :, o[6]:].astype(MXU_DTYPE)
    wb0 = w_branch[:, 0].reshape(depth, N_HEADS, HEAD_DIM, d_model)[:, perm].reshape(depth, -1, d_model)
    wb0 = wb0.astype(MXU_DTYPE)
    wb1 = w_branch[:, 1].astype(MXU_DTYPE)
    wo = w_out.astype(MXU_DTYPE)
    wr32 = jnp.pad(w_router.astype(F32), ((0, 0), (0, LANES - N_EXPERTS)))
    wr_hi = wr32.astype(MXU_DTYPE)
    wr = jnp.stack([wr_hi, (wr32 - wr_hi.astype(F32)).astype(MXU_DTYPE)])
    rb = jnp.pad(router_bias.astype(F32), (0, LANES - N_EXPERTS)).reshape(1, LANES)
    wg = w_exp_gate.astype(MXU_DTYPE)
    wu = w_exp_up.astype(MXU_DTYPE)
    wd = w_exp_down.astype(MXU_DTYPE)
    ln1g, ln1b, ln2g, ln2b = (a.reshape(depth, 1, d_model) for a in (ln1_g, ln1_b, ln2_g, ln2_b))
    cbias = conv_b.reshape(depth, 1, CONV_DIM)

    cos_p, sin_p = _rope_tables(jnp.arange(seq, dtype=jnp.int32))
    pos_s = jnp.tile(past + jnp.arange(t_new, dtype=jnp.int32), n_req)
    cos_s, sin_s = _rope_tables(pos_s)
    ck_t = jnp.transpose(cache_k, (0, 1, 3, 4, 2))
    cv_t = jnp.transpose(cache_v, (0, 1, 3, 4, 2))
    cik_t = jnp.transpose(cache_ik, (0, 1, 3, 2))
    zeros_state = jnp.zeros((depth, n_req, t_new, CONV_DIM), F32)
    sp1 = zeros_state.at[:, :, 0].set(state_conv[:, :, 1]).reshape(depth, n_req * t_new, CONV_DIM)
    sp2 = zeros_state.at[:, :, :2].set(state_conv).reshape(depth, n_req * t_new, CONV_DIM)

    qb = _block(seq, PROMPT_QUERY_BLOCK)
    kc = _block(qb, PROMPT_KEY_CHUNK)
    tri = (np.arange(kc)[:, None] < np.arange(kc)[None, :])
    tril_p = jnp.asarray(tri.T, MXU_DTYPE)
    tri_s = jnp.asarray(tri[:LANES, :LANES], MXU_DTYPE)
    tm_p = _block(seq, 512)
    n_s = n_req * t_new
    tm_s = _block(n_s, 512)
    rps = next(r for r in (4, 2, 1) if n_req % r == 0)

    xp = x_prompt.reshape(batch * seq, d_model)
    xs = x_sample.reshape(n_s, d_model)
    outs = [[] for _ in range(8)]
    for l in range(depth):
        q, k, v, iq, ik, kb, vt, ikb, iwt = _attn_proj(xp, wa, l, cos_p, sin_p, tm_p, wiw_t=wiw_t, chunk=kc)
        a = _prompt_attn(q, iq, iwt, kb, vt, ikb, tril_p, batch, seq, qb, kc, k_prompt_sel)
        x1, gates, u_last = _mix(xp, a, wm, conv_w, cbias, wb0, wb1, wo, ln1g, ln1b, wr, rb, l, tm_p, alpha, seq=seq)
        xp = _moe(x1, gates, wg, wu, wd, ln2g, ln2b, l, tm_p, alpha)
        outs[0].append(k.reshape(batch, seq, N_KV_HEADS, HEAD_DIM))
        outs[1].append(v.reshape(batch, seq, N_KV_HEADS, HEAD_DIM))
        outs[2].append(ik.reshape(batch, seq, IDX_DIM))
        outs[3].append(u_last[:, SUBLANES - (CONV_W - 1):])

        q, k, v, iq, ik, iw = _attn_proj(xs, wa, l, cos_s, sin_s, tm_s)
        a = _sample_attn(page_table, q, iq, iw, k, v, ik, tri_s, ck_t, cv_t, cik_t, l, rps, t_new, k_sample_sel)
        x1, gates, u = _mix(xs, a, wm, conv_w, cbias, wb0, wb1, wo, ln1g, ln1b, wr, rb, l, tm_s, alpha,
                            sp1=sp1, sp2=sp2, t_new=t_new)
        xs = _moe(x1, gates, wg, wu, wd, ln2g, ln2b, l, tm_s, alpha)
        outs[4].append(k.reshape(n_req, t_new, N_KV_HEADS, HEAD_DIM))
        outs[5].append(v.reshape(n_req, t_new, N_KV_HEADS, HEAD_DIM))
        outs[6].append(ik.reshape(n_req, t_new, IDX_DIM))
        outs[7].append(u.reshape(n_req, t_new, CONV_DIM)[:, t_new - (CONV_W - 1):])
    stacked = [jnp.stack(o_) for o_ in outs]
    return (xp.reshape(batch, seq, d_model), xs.reshape(n_req, t_new, d_model), *stacked)
```
